```python
import math
import jax
import jax.numpy as jnp
from jax import lax
import numpy as np

D_MODEL = 4096
BATCH = 2
SEQ = 8192
DEPTH = 2

F32 = jnp.float32
N_MIXERS = 4
HEAD_DIM = 128
HEADS_PER_MIXER = D_MODEL // (N_MIXERS * HEAD_DIM)
GROUP_WIDTH = HEADS_PER_MIXER * HEAD_DIM
MIX_WIDTH = N_MIXERS * GROUP_WIDTH
DIFF_DIM = HEAD_DIM // 2
D_FF = 11008
ROPE_THETA = 10000.0
RET_THETA = 10000.0
Q_BLOCK = 128
CHUNK = 128
CONV_WIDTH = 4
NORM_EPS = 1e-6
N_NORMS = 6
IN_SIZES = (GROUP_WIDTH, GROUP_WIDTH, GROUP_WIDTH, HEADS_PER_MIXER,
            GROUP_WIDTH, GROUP_WIDTH, GROUP_WIDTH,
            GROUP_WIDTH, GROUP_WIDTH, GROUP_WIDTH, GROUP_WIDTH,
            GROUP_WIDTH, GROUP_WIDTH, GROUP_WIDTH, GROUP_WIDTH, HEADS_PER_MIXER, HEADS_PER_MIXER)
D_IN = 14 * GROUP_WIDTH + 3 * HEADS_PER_MIXER

kernel_name = 'hybrid_parallel_heads_macaron_trunk'


def rms_norm(x, gain=None):
    xf = x.astype(F32)
    y = xf * lax.rsqrt(jnp.mean(xf * xf, axis=-1, keepdims=True) + NORM_EPS)
    if gain is not None:
        y = y * gain.astype(F32)
    return y.astype(x.dtype)


def to_heads(t, n_heads):
    b, s, _ = t.shape
    return t.reshape(b, s, n_heads, -1).transpose(0, 2, 1, 3)


def from_heads(t):
    b, h, s, d = t.shape
    return t.transpose(0, 2, 1, 3).reshape(b, s, h * d)


def swiglu(x, w_gate, w_up, w_down):
    return (jax.nn.silu(x @ w_gate) * (x @ w_up)) @ w_down


def rotary(x):
    s, d = x.shape[2], x.shape[3]
    inv = 1.0 / (ROPE_THETA ** (jnp.arange(0, d, 2, dtype=F32) / d))
    ang = jnp.arange(s, dtype=F32)[:, None] * inv[None, :]
    cos, sin = jnp.cos(ang), jnp.sin(ang)
    xf = x.astype(F32)
    x1, x2 = xf[..., : d // 2], xf[..., d // 2:]
    return jnp.concatenate([x1 * cos - x2 * sin, x2 * cos + x1 * sin], axis=-1).astype(x.dtype)


def retnet_rotate(x):
    s, d = x.shape[2], x.shape[3]
    inv = 1.0 / (RET_THETA ** jnp.linspace(0.0, 1.0, d // 2, dtype=F32))
    ang = jnp.arange(s, dtype=F32)[:, None] * inv[None, :]
    cos, sin = jnp.cos(ang), jnp.sin(ang)
    xf = x.astype(F32)
    xe, xo = xf[..., 0::2], xf[..., 1::2]
    out = jnp.stack([xe * cos - xo * sin, xo * cos + xe * sin], axis=-1)
    return out.reshape(x.shape).astype(x.dtype)


def causal_block_softmax_attention(q, k, v, cum_log_f=None):
    b, h, s, d = q.shape
    nb = s // Q_BLOCK
    scale = d ** -0.5
    q_blocks = q.reshape(b, h, nb, Q_BLOCK, d).transpose(2, 0, 1, 3, 4)
    starts = jnp.arange(nb, dtype=jnp.int32) * Q_BLOCK
    key_pos = jnp.arange(s, dtype=jnp.int32)

    def block(q_blk, start, c_blk):
        logits = jnp.einsum('bhqd,bhkd->bhqk', q_blk, k).astype(F32) * scale
        if c_blk is not None:
            logits = logits + c_blk[..., :, None] - cum_log_f[..., None, :]
        q_pos = start + jnp.arange(Q_BLOCK, dtype=jnp.int32)
        logits = jnp.where(key_pos[None, :] <= q_pos[:, None], logits, -jnp.inf)
        p = jax.nn.softmax(logits, axis=-1)
        return jnp.einsum('bhqk,bhkd->bhqd', p.astype(v.dtype), v)

    if cum_log_f is None:
        out = lax.map(lambda a: block(a[0], a[1], None), (q_blocks, starts))
    else:
        c_blocks = cum_log_f.reshape(b, h, nb, Q_BLOCK).transpose(2, 0, 1, 3)
        out = lax.map(lambda a: block(a[0], a[1], a[2]), (q_blocks, starts, c_blocks))
    return out.transpose(1, 2, 0, 3, 4).reshape(b, h, s, v.shape[-1])


def to_chunks(t):
    b, h, s, d = t.shape
    return t.astype(F32).reshape(b, h, s // CHUNK, CHUNK, d).transpose(2, 0, 1, 3, 4)


def from_chunks(t):
    nc, b, h, l, d = t.shape
    return t.transpose(1, 2, 0, 3, 4).reshape(b, h, nc * l, d)


def retention_chunkwise(q, k, v):
    b, h, s, dk = q.shape
    dv = v.shape[-1]
    log_gamma = jnp.log(1.0 - 2.0 ** (-5.0 - jnp.arange(h, dtype=F32)))
    pos = jnp.arange(CHUNK, dtype=F32)
    rel = pos[:, None] - pos[None, :]
    intra = jnp.where(rel >= 0, jnp.exp(jnp.maximum(rel, 0.0) * log_gamma[:, None, None]), 0.0)
    q_decay = jnp.exp((pos + 1.0)[None, :] * log_gamma[:, None])[..., None]
    k_decay = jnp.exp((CHUNK - 1.0 - pos)[None, :] * log_gamma[:, None])[..., None]
    chunk_decay = jnp.exp(CHUNK * log_gamma)[:, None, None]
    qc, kc, vc = to_chunks(q), to_chunks(k * (dk ** -0.5)), to_chunks(v)

    def step(state, inp):
        qi, ki, vi = inp
        scores = jnp.einsum('bhld,bhmd->bhlm', qi, ki) * intra
        o = (jnp.einsum('bhlm,bhme->bhle', scores, vi)
             + jnp.einsum('bhld,bhde->bhle', qi, state) * q_decay)
        state = chunk_decay * state + jnp.einsum('bhmd,bhme->bhde', ki * k_decay, vi)
        return state, o

    state0 = jnp.zeros((b, h, dk, dv), F32)
    _, out = lax.scan(step, state0, (qc, kc, vc))
    return from_chunks(out).astype(v.dtype)


def mlstm_chunkwise(q, k, v, log_i, log_f):
    b, h, s, d = q.shape
    nc = s // CHUNK
    causal = jnp.tril(jnp.ones((CHUNK, CHUNK), dtype=bool))
    qc, kc, vc = to_chunks(q), to_chunks(k * (d ** -0.5)), to_chunks(v)
    ic = log_i.astype(F32).reshape(b, h, nc, CHUNK).transpose(2, 0, 1, 3)
    fc = log_f.astype(F32).reshape(b, h, nc, CHUNK).transpose(2, 0, 1, 3)

    def step(carry, inp):
        c_mat, n_vec, m = carry
        qi, ki, vi, li, lf = inp
        bcum = jnp.cumsum(lf, axis=-1)
        dlog = bcum[..., :, None] - bcum[..., None, :] + li[..., None, :]
        dlog = jnp.where(causal, dlog, -jnp.inf)
        inter = bcum + m[..., None]
        m_t = jnp.maximum(inter, jnp.max(dlog, axis=-1))
        w_intra = jnp.exp(dlog - m_t[..., None])
        w_inter = jnp.exp(inter - m_t)
        s_qk = jnp.einsum('bhld,bhmd->bhlm', qi, ki) * w_intra
        num = (jnp.einsum('bhlm,bhme->bhle', s_qk, vi)
               + w_inter[..., None] * jnp.einsum('bhld,bhde->bhle', qi, c_mat))
        den = jnp.sum(s_qk, axis=-1) + w_inter * jnp.einsum('bhld,bhd->bhl', qi, n_vec)
        h_t = num / jnp.maximum(jnp.abs(den), jnp.exp(-m_t))[..., None]
        b_last = bcum[..., -1]
        end_log = b_last[..., None] - bcum + li
        m_new = jnp.maximum(b_last + m, jnp.max(end_log, axis=-1))
        w_end = jnp.exp(end_log - m_new[..., None])
        carry_decay = jnp.exp(b_last + m - m_new)
        c_mat = carry_decay[..., None, None] * c_mat + jnp.einsum('bhm,bhmd,bhme->bhde', w_end, ki, vi)
        n_vec = carry_decay[..., None] * n_vec + jnp.einsum('bhm,bhmd->bhd', w_end, ki)
        return (c_mat, n_vec, m_new), h_t

    carry0 = (jnp.zeros((b, h, d, d), F32), jnp.zeros((b, h, d), F32), jnp.zeros((b, h), F32))
    _, out = lax.scan(step, carry0, (qc, kc, vc, ic, fc))
    return from_chunks(out).astype(v.dtype)


def causal_depthwise_conv(x, w):
    c = x.shape[-1]
    return lax.conv_general_dilated(
        x, w[:, None, :].astype(x.dtype), window_strides=(1,),
        padding=[(CONV_WIDTH - 1, 0)], dimension_numbers=('NWC', 'WIO', 'NWC'),
        feature_group_count=c)


def hybrid_mixer(hn, w_in, w_out, fox_fgate_b, diff_lq1, diff_lk1, diff_lq2, diff_lk2,
                 diff_subln_g, mlstm_conv_w, mlstm_igate_b, mlstm_fgate_b, mlstm_norm_g, lam_init):
    b, s, _ = hn.shape
    H = HEADS_PER_MIXER
    split_points = [int(p) for p in np.cumsum(IN_SIZES)[:-1]]
    proj = hn @ w_in
    (fq, fk, fv, ff, dq, dkey, dv, rq, rk, rv, rg,
     mq, mk, mv, mo, mi, mf) = jnp.split(proj, split_points, axis=-1)

    log_fg = jax.nn.log_sigmoid((ff + fox_fgate_b).astype(F32))
    cum = jnp.cumsum(log_fg, axis=1).transpose(0, 2, 1)
    fox = causal_block_softmax_attention(to_heads(fq, H), to_heads(fk, H), to_heads(fv, H), cum)
    fox_out = from_heads(fox)

    def diff_heads(t):
        return t.reshape(b, s, H, 2, DIFF_DIM).transpose(0, 3, 2, 1, 4).reshape(b, 2 * H, s, DIFF_DIM)
    q12 = rotary(diff_heads(dq))
    k12 = rotary(diff_heads(dkey))
    vd = to_heads(dv, H)
    o12 = causal_block_softmax_attention(q12, k12, jnp.concatenate([vd, vd], axis=1))
    o12 = o12.reshape(b, 2, H, s, HEAD_DIM)
    lam = (jnp.exp(jnp.sum(diff_lq1.astype(F32) * diff_lk1.astype(F32)))
           - jnp.exp(jnp.sum(diff_lq2.astype(F32) * diff_lk2.astype(F32))) + lam_init)
    diff = o12[:, 0] - lam.astype(o12.dtype) * o12[:, 1]
    diff = rms_norm(diff, diff_subln_g) * (1.0 - lam_init)
    diff_out = from_heads(diff)

    ret = retention_chunkwise(retnet_rotate(to_heads(rq, H)), retnet_rotate(to_heads(rk, H)), to_heads(rv, H))
    ret_out = jax.nn.silu(rg) * from_heads(rms_norm(ret))

    qk = jax.nn.silu(causal_depthwise_conv(jnp.concatenate([mq, mk], axis=-1), mlstm_conv_w))
    mq2, mk2 = qk[..., :GROUP_WIDTH], qk[..., GROUP_WIDTH:]
    log_i = (mi + mlstm_igate_b).astype(F32).transpose(0, 2, 1)
    log_f = jax.nn.log_sigmoid((mf + mlstm_fgate_b).astype(F32)).transpose(0, 2, 1)
    hm = mlstm_chunkwise(to_heads(mq2, H), to_heads(mk2, H), to_heads(mv, H), log_i, log_f)
    hm = rms_norm(hm, mlstm_norm_g.reshape(H, 1, HEAD_DIM))
    mlstm_out = jax.nn.sigmoid(mo) * from_heads(hm)

    mixed = jnp.concatenate([fox_out, diff_out, ret_out, mlstm_out], axis=-1)
    return mixed @ w_out


def setup_inputs(seed: int = 0) -> dict:
    key = jax.random.key(seed)
    ks = jax.random.split(key, 20)
    H = HEADS_PER_MIXER

    def nrm(k, shape, scale):
        return jax.random.normal(k, shape, F32) * scale

    return {
        'x': nrm(ks[0], (BATCH, SEQ, D_MODEL), 1.0),
        'norm_gains': 1.0 + nrm(ks[1], (DEPTH, N_NORMS, D_MODEL), 0.02),
        'ffn_a_gate': nrm(ks[2], (DEPTH, D_MODEL, D_FF), D_MODEL ** -0.5),
        'ffn_a_up': nrm(ks[3], (DEPTH, D_MODEL, D_FF), D_MODEL ** -0.5),
        'ffn_a_down': nrm(ks[4], (DEPTH, D_FF, D_MODEL), D_FF ** -0.5),
        'w_in': nrm(ks[5], (DEPTH, D_MODEL, D_IN), D_MODEL ** -0.5),
        'w_out': nrm(ks[6], (DEPTH, MIX_WIDTH, D_MODEL), MIX_WIDTH ** -0.5),
        'fox_fgate_b': nrm(ks[7], (DEPTH, H), 0.1),
        'diff_lq1': nrm(ks[8], (DEPTH, DIFF_DIM), 0.1),
        'diff_lk1': nrm(ks[9], (DEPTH, DIFF_DIM), 0.1),
        'diff_lq2': nrm(ks[10], (DEPTH, DIFF_DIM), 0.1),
        'diff_lk2': nrm(ks[11], (DEPTH, DIFF_DIM), 0.1),
        'diff_subln_g': 1.0 + nrm(ks[12], (DEPTH, HEAD_DIM), 0.02),
        'mlstm_conv_w': nrm(ks[13], (DEPTH, CONV_WIDTH, 2 * GROUP_WIDTH), CONV_WIDTH ** -0.5),
        'mlstm_igate_b': nrm(ks[14], (DEPTH, H), 0.1),
        'mlstm_fgate_b': jnp.linspace(3.0, 6.0, H, dtype=F32)[None, :] + nrm(ks[15], (DEPTH, H), 0.1),
        'mlstm_norm_g': 1.0 + nrm(ks[16], (DEPTH, GROUP_WIDTH), 0.02),
        'ffn_b_gate': nrm(ks[17], (DEPTH, D_MODEL, D_FF), D_MODEL ** -0.5),
        'ffn_b_up': nrm(ks[18], (DEPTH, D_MODEL, D_FF), D_MODEL ** -0.5),
        'ffn_b_down': nrm(ks[19], (DEPTH, D_FF, D_MODEL), D_FF ** -0.5),
    }


def reference(x, norm_gains, ffn_a_gate, ffn_a_up, ffn_a_down, w_in, w_out, fox_fgate_b,
              diff_lq1, diff_lk1, diff_lq2, diff_lk2, diff_subln_g, mlstm_conv_w,
              mlstm_igate_b, mlstm_fgate_b, mlstm_norm_g, ffn_b_gate, ffn_b_up, ffn_b_down):
    for l in range(DEPTH):
        lam_init = 0.8 - 0.6 * math.exp(-0.3 * l)
        g = norm_gains[l]
        x = x + 0.5 * rms_norm(swiglu(rms_norm(x, g[0]), ffn_a_gate[l], ffn_a_up[l], ffn_a_down[l]), g[1])
        mix = hybrid_mixer(rms_norm(x, g[2]), w_in[l], w_out[l], fox_fgate_b[l],
                           diff_lq1[l], diff_lk1[l], diff_lq2[l], diff_lk2[l], diff_subln_g[l],
                           mlstm_conv_w[l], mlstm_igate_b[l], mlstm_fgate_b[l], mlstm_norm_g[l], lam_init)
        x = x + rms_norm(mix, g[3])
        x = x + 0.5 * rms_norm(swiglu(rms_norm(x, g[4]), ffn_b_gate[l], ffn_b_up[l], ffn_b_down[l]), g[5])
    return x
```

```python
import functools
import math

import jax
import jax.numpy as jnp
import numpy as np
from jax import lax
from jax.experimental import pallas as pl
from jax.experimental.pallas import tpu as pltpu

F32 = jnp.float32
BF16 = jnp.bfloat16

N_MIXERS = 4
HEAD_DIM = 128
HEADS = 8
GROUP = HEADS * HEAD_DIM
DIFF_DIM = HEAD_DIM // 2
CHUNK = 128
CONV_WIDTH = 4
NORM_EPS = 1e-6
ROPE_THETA = 10000.0
RET_THETA = 10000.0
N_GROUPS = 14
GATE_ROWS = 32
NEG_BIG = -1e30

G_FQ, G_FK, G_FV, G_DQ, G_DK, G_DV, G_RQ, G_RK, G_RV, G_RG, G_MQ, G_MK, G_MV, G_MO = range(14)

VMEM_LIMIT_BYTES = 56 * 1024 * 1024

TILES = dict(
    norm_rows=256,
    ffn_up=(1024, 256),
    ffn_down=(512, 256),
    in_proj=(1024, 512),
    out_proj=(1024, 256),
    gates=1024,
    rotary=512,
    attn_q=512,
    recurrent=1024,
)


def _tile(dim, want):
    t = min(dim, want)
    while dim % t:
        t -= 128 if t > 128 else 8
    return t


def _params(*sem):
    return pltpu.CompilerParams(dimension_semantics=sem, vmem_limit_bytes=VMEM_LIMIT_BYTES)


def _rms(x):
    return x * lax.rsqrt(jnp.mean(x * x, axis=-1, keepdims=True) + NORM_EPS)


def _norm_kernel(x_ref, g_ref, o_ref):
    o_ref[...] = (_rms(x_ref[...]) * g_ref[...]).astype(o_ref.dtype)


def rms_norm_rows(x, gain):
    t, d = x.shape
    tm = _tile(t, TILES["norm_rows"])
    return pl.pallas_call(
        _norm_kernel,
        grid=(t // tm,),
        in_specs=[pl.BlockSpec((tm, d), lambda i: (i, 0)),
                  pl.BlockSpec((1, d), lambda i: (0, 0))],
        out_specs=pl.BlockSpec((tm, d), lambda i: (i, 0)),
        out_shape=jax.ShapeDtypeStruct((t, d), BF16),
        compiler_params=_params("parallel"),
        name="rms_norm_rows",
    )(x, gain.reshape(1, d))


def _resid_norm_kernel(x_ref, y_ref, gp_ref, gn_ref, xo_ref, xn_ref, *, coef):
    yn = _rms(y_ref[...].astype(F32)) * gp_ref[...]
    x = x_ref[...] + coef * yn
    xo_ref[...] = x
    xn_ref[...] = (_rms(x) * gn_ref[...]).astype(xn_ref.dtype)


def _resid_kernel(x_ref, y_ref, gp_ref, xo_ref, *, coef):
    yn = _rms(y_ref[...].astype(F32)) * gp_ref[...]
    xo_ref[...] = x_ref[...] + coef * yn


def resid_norm(x, y, g_post, g_next, coef):
    t, d = x.shape
    tm = _tile(t, TILES["norm_rows"])
    row = pl.BlockSpec((tm, d), lambda i: (i, 0))
    vec = pl.BlockSpec((1, d), lambda i: (0, 0))
    if g_next is None:
        return pl.pallas_call(
            functools.partial(_resid_kernel, coef=coef),
            grid=(t // tm,),
            in_specs=[row, row, vec],
            out_specs=row,
            out_shape=jax.ShapeDtypeStruct((t, d), F32),
            compiler_params=_params("parallel"),
            name="resid_last",
        )(x, y, g_post.reshape(1, d)), None
    return pl.pallas_call(
        functools.partial(_resid_norm_kernel, coef=coef),
        grid=(t // tm,),
        in_specs=[row, row, vec, vec],
        out_specs=[row, row],
        out_shape=[jax.ShapeDtypeStruct((t, d), F32), jax.ShapeDtypeStruct((t, d), BF16)],
        compiler_params=_params("parallel"),
        name="resid_norm",
    )(x, y, g_post.reshape(1, d), g_next.reshape(1, d))


def _ffn_up_kernel(x_ref, wg_ref, wu_ref, o_ref):
    x = x_ref[...]
    g = jnp.dot(x, wg_ref[...], preferred_element_type=F32)
    u = jnp.dot(x, wu_ref[...], preferred_element_type=F32)
    o_ref[...] = (g * jax.nn.sigmoid(g) * u).astype(o_ref.dtype)


def ffn_up(xn, w_gate, w_up):
    t, d = xn.shape
    f = w_gate.shape[1]
    bm, bn = _tile(t, TILES["ffn_up"][0]), _tile(f, TILES["ffn_up"][1])
    return pl.pallas_call(
        _ffn_up_kernel,
        grid=(t // bm, f // bn),
        in_specs=[pl.BlockSpec((bm, d), lambda i, j: (i, 0)),
                  pl.BlockSpec((d, bn), lambda i, j: (0, j)),
                  pl.BlockSpec((d, bn), lambda i, j: (0, j))],
        out_specs=pl.BlockSpec((bm, bn), lambda i, j: (i, j)),
        out_shape=jax.ShapeDtypeStruct((t, f), BF16),
        compiler_params=_params("parallel", "arbitrary"),
        name="ffn_up",
    )(xn, w_gate, w_up)


def _matmul_kernel(a_ref, w_ref, o_ref):
    o_ref[...] = jnp.dot(a_ref[...], w_ref[...], preferred_element_type=F32).astype(o_ref.dtype)


def matmul(a, w, name):
    m, k = a.shape
    n = w.shape[1]
    bm, bn = _tile(m, TILES[name][0]), _tile(n, TILES[name][1])
    return pl.pallas_call(
        _matmul_kernel,
        grid=(m // bm, n // bn),
        in_specs=[pl.BlockSpec((bm, k), lambda i, j: (i, 0)),
                  pl.BlockSpec((k, bn), lambda i, j: (0, j))],
        out_specs=pl.BlockSpec((bm, bn), lambda i, j: (i, j)),
        out_shape=jax.ShapeDtypeStruct((m, n), BF16),
        compiler_params=_params("parallel", "arbitrary"),
        name=name,
    )(a, w)


def _out_proj_kernel(a0_ref, a1_ref, a2_ref, a3_ref, w_ref, o_ref):
    acc = jnp.dot(a0_ref[...], w_ref[0], preferred_element_type=F32)
    acc += jnp.dot(a1_ref[...], w_ref[1], preferred_element_type=F32)
    acc += jnp.dot(a2_ref[...], w_ref[2], preferred_element_type=F32)
    acc += jnp.dot(a3_ref[...], w_ref[3], preferred_element_type=F32)
    o_ref[...] = acc.astype(o_ref.dtype)


def out_proj(parts, w_out4):
    t, gw = parts[0].shape
    n = w_out4.shape[2]
    bm, bn = _tile(t, TILES["out_proj"][0]), _tile(n, TILES["out_proj"][1])
    a_spec = pl.BlockSpec((bm, gw), lambda i, j: (i, 0))
    return pl.pallas_call(
        _out_proj_kernel,
        grid=(t // bm, n // bn),
        in_specs=[a_spec, a_spec, a_spec, a_spec,
                  pl.BlockSpec((N_MIXERS, gw, bn), lambda i, j: (0, 0, j))],
        out_specs=pl.BlockSpec((bm, bn), lambda i, j: (i, j)),
        out_shape=jax.ShapeDtypeStruct((t, n), BF16),
        compiler_params=_params("parallel", "arbitrary"),
        name="out_proj",
    )(*parts, w_out4)


def _log_sigmoid(z):
    return jnp.minimum(z, 0.0) - jnp.log(1.0 + jnp.exp(-jnp.abs(z)))


def _chunk_cumsum_lanes(x):
    lane = lax.broadcasted_iota(jnp.int32, x.shape, 1) & (CHUNK - 1)
    s = 1
    while s < CHUNK:
        x = x + jnp.where(lane >= s, pltpu.roll(x, s, 1), 0.0)
        s *= 2
    return x


def _gates_kernel(hn_ref, wt_ref, b_ref, o_ref, carry_sc):
    @pl.when(pl.program_id(1) == 0)
    def _():
        carry_sc[...] = jnp.zeros_like(carry_sc)

    z = lax.dot_general(wt_ref[...], hn_ref[...], (((1,), (1,)), ((), ())),
                        preferred_element_type=F32) + b_ref[...]
    tm = z.shape[1]
    fox = _chunk_cumsum_lanes(_log_sigmoid(z[0:HEADS]))
    carry = carry_sc[...]
    for c in range(tm // CHUNK):
        seg = fox[:, c * CHUNK:(c + 1) * CHUNK] + carry
        o_ref[0:HEADS, c * CHUNK:(c + 1) * CHUNK] = seg
        carry = jnp.broadcast_to(seg[:, CHUNK - 1:CHUNK], carry.shape)
    carry_sc[...] = carry
    o_ref[HEADS:2 * HEADS, :] = z[HEADS:2 * HEADS]
    o_ref[2 * HEADS:3 * HEADS, :] = _chunk_cumsum_lanes(_log_sigmoid(z[2 * HEADS:3 * HEADS]))
    o_ref[3 * HEADS:, :] = jnp.zeros((GATE_ROWS - 3 * HEADS, tm), F32)


def gates(hn, w_gates_t, gate_bias, batch):
    t, d = hn.shape
    s = t // batch
    tm = _tile(s, TILES["gates"])
    ns = s // tm
    return pl.pallas_call(
        _gates_kernel,
        grid=(batch, ns),
        in_specs=[pl.BlockSpec((tm, d), lambda b, i: (b * ns + i, 0)),
                  pl.BlockSpec((GATE_ROWS, d), lambda b, i: (0, 0)),
                  pl.BlockSpec((GATE_ROWS, 1), lambda b, i: (0, 0))],
        out_specs=pl.BlockSpec((GATE_ROWS, tm), lambda b, i: (0, b * ns + i)),
        out_shape=jax.ShapeDtypeStruct((GATE_ROWS, t), F32),
        scratch_shapes=[pltpu.VMEM((HEADS, CHUNK), F32)],
        compiler_params=_params("parallel", "arbitrary"),
        name="gates",
    )(hn, w_gates_t, gate_bias)


def _rotate(x, cos, sin_signed):
    return x * cos + pltpu.roll(x, HEAD_DIM // 2, 1) * sin_signed


def _rotary_kernel(q_ref, k_ref, cos_ref, sin_ref, qo_ref, ko_ref, *, q_scale):
    cos, sin = cos_ref[...], sin_ref[...]
    for h in range(HEADS):
        sl = slice(h * HEAD_DIM, (h + 1) * HEAD_DIM)
        qo_ref[:, sl] = (_rotate(q_ref[:, sl].astype(F32), cos, sin) * q_scale).astype(qo_ref.dtype)
        ko_ref[:, sl] = _rotate(k_ref[:, sl].astype(F32), cos, sin).astype(ko_ref.dtype)


def rotary_qk(proj, gq, gk, cos, sin, batch, q_scale):
    t = proj.shape[0]
    s = t // batch
    tm = _tile(s, TILES["rotary"])
    ns = s // tm
    tab = pl.BlockSpec((tm, HEAD_DIM), lambda i: (i % ns, 0))
    out = pl.BlockSpec((tm, GROUP), lambda i: (i, 0))
    return pl.pallas_call(
        functools.partial(_rotary_kernel, q_scale=q_scale),
        grid=(t // tm,),
        in_specs=[pl.BlockSpec((tm, GROUP), lambda i: (i, gq)),
                  pl.BlockSpec((tm, GROUP), lambda i: (i, gk)), tab, tab],
        out_specs=[out, out],
        out_shape=[jax.ShapeDtypeStruct((t, GROUP), BF16)] * 2,
        compiler_params=_params("parallel"),
        name="rotary_qk",
    )(proj, proj, cos, sin)


def _online_softmax_step(s, v, m_sc, l_sc, acc_sc):
    m_old = m_sc[...]
    m_new = jnp.maximum(m_old, jnp.max(s, axis=-1, keepdims=True))
    alpha = jnp.exp(m_old - m_new)
    p = jnp.exp(s - m_new)
    l_sc[...] = alpha * l_sc[...] + jnp.sum(p, axis=-1, keepdims=True)
    acc_sc[...] = alpha * acc_sc[...] + jnp.dot(p.astype(v.dtype), v, preferred_element_type=F32)
    m_sc[...] = m_new


def _qk(q, k):
    return lax.dot_general(q, k, (((1,), (1,)), ((), ())), preferred_element_type=F32)


def _fox_kernel(q_ref, k_ref, v_ref, c_ref, o_ref, m_sc, l_sc, acc_sc, *, tq, scale):
    qi = pl.program_id(2)
    q = (q_ref[...].astype(F32) * scale).astype(q_ref.dtype)
    c0 = c_ref[:, pl.ds(pl.multiple_of(qi * tq, tq), CHUNK)][:, 0:1]
    m_sc[...] = jnp.full_like(m_sc, NEG_BIG)
    l_sc[...] = jnp.zeros_like(l_sc)
    acc_sc[...] = jnp.zeros_like(acc_sc)

    def logits(j):
        start = pl.multiple_of(j * tq, tq)
        k = k_ref[pl.ds(start, tq), :]
        return _qk(q, k) + (c0 - c_ref[:, pl.ds(start, tq)]), v_ref[pl.ds(start, tq), :]

    def body(j, carry):
        s, v = logits(j)
        _online_softmax_step(s, v, m_sc, l_sc, acc_sc)
        return carry

    lax.fori_loop(0, qi, body, 0)
    s, v = logits(qi)
    row = lax.broadcasted_iota(jnp.int32, s.shape, 0)
    col = lax.broadcasted_iota(jnp.int32, s.shape, 1)
    _online_softmax_step(jnp.where(col <= row, s, NEG_BIG), v, m_sc, l_sc, acc_sc)
    o_ref[...] = (acc_sc[...] / l_sc[...]).astype(o_ref.dtype)


def fox_attention(proj, cum3, batch):
    t = proj.shape[0]
    s = t // batch
    tq = _tile(s, TILES["attn_q"])
    nq = s // tq
    return pl.pallas_call(
        functools.partial(_fox_kernel, tq=tq, scale=HEAD_DIM ** -0.5),
        grid=(batch, HEADS, nq),
        in_specs=[pl.BlockSpec((tq, HEAD_DIM), lambda b, h, i: (b * nq + i, G_FQ * HEADS + h)),
                  pl.BlockSpec((s, HEAD_DIM), lambda b, h, i: (b, G_FK * HEADS + h)),
                  pl.BlockSpec((s, HEAD_DIM), lambda b, h, i: (b, G_FV * HEADS + h)),
                  pl.BlockSpec((None, 1, s), lambda b, h, i: (h, 0, b))],
        out_specs=pl.BlockSpec((tq, HEAD_DIM), lambda b, h, i: (b * nq + i, h)),
        out_shape=jax.ShapeDtypeStruct((t, GROUP), BF16),
        scratch_shapes=[pltpu.VMEM((tq, 1), F32), pltpu.VMEM((tq, 1), F32),
                        pltpu.VMEM((tq, HEAD_DIM), F32)],
        compiler_params=_params("parallel", "parallel", "arbitrary"),
        name="fox_attention",
    )(proj, proj, proj, cum3)


def _diff_kernel(q_ref, k_ref, v_ref, lam_ref, g_ref, o_ref, m_sc, l_sc, acc_sc, *, tq, lam_init):
    qi = pl.program_id(2)
    q = q_ref[...]
    lane = lax.broadcasted_iota(jnp.int32, q.shape, 1)
    first_map = (lane & (DIFF_DIM // 2)) == 0
    zero = jnp.zeros_like(q)
    q12 = jnp.concatenate([jnp.where(first_map, q, zero), jnp.where(first_map, zero, q)], axis=0)
    m_sc[...] = jnp.full_like(m_sc, NEG_BIG)
    l_sc[...] = jnp.zeros_like(l_sc)
    acc_sc[...] = jnp.zeros_like(acc_sc)

    def body(j, carry):
        start = pl.multiple_of(j * tq, tq)
        _online_softmax_step(_qk(q12, k_ref[pl.ds(start, tq), :]), v_ref[pl.ds(start, tq), :],
                             m_sc, l_sc, acc_sc)
        return carry

    lax.fori_loop(0, qi, body, 0)
    start = pl.multiple_of(qi * tq, tq)
    s = _qk(q12, k_ref[pl.ds(start, tq), :])
    row = lax.broadcasted_iota(jnp.int32, s.shape, 0)
    row = jnp.where(row >= tq, row - tq, row)
    col = lax.broadcasted_iota(jnp.int32, s.shape, 1)
    _online_softmax_step(jnp.where(col <= row, s, NEG_BIG), v_ref[pl.ds(start, tq), :], m_sc, l_sc, acc_sc)

    o = acc_sc[...] / l_sc[...]
    lp = lam_ref[...]
    lam = (jnp.exp(jnp.sum(lp[0:1] * lp[1:2], axis=-1, keepdims=True))
           - jnp.exp(jnp.sum(lp[2:3] * lp[3:4], axis=-1, keepdims=True)) + lam_init)
    d = o[:tq] - lam * o[tq:]
    o_ref[...] = (_rms(d) * g_ref[...] * (1.0 - lam_init)).astype(o_ref.dtype)


def diff_attention(dq, dk, proj, lam_params, subln_g, batch, lam_init):
    t = proj.shape[0]
    s = t // batch
    tq = _tile(s, TILES["attn_q"])
    nq = s // tq
    return pl.pallas_call(
        functools.partial(_diff_kernel, tq=tq, lam_init=lam_init),
        grid=(batch, HEADS, nq),
        in_specs=[pl.BlockSpec((tq, HEAD_DIM), lambda b, h, i: (b * nq + i, h)),
                  pl.BlockSpec((s, HEAD_DIM), lambda b, h, i: (b, h)),
                  pl.BlockSpec((s, HEAD_DIM), lambda b, h, i: (b, G_DV * HEADS + h)),
                  pl.BlockSpec((4, DIFF_DIM), lambda b, h, i: (0, 0)),
                  pl.BlockSpec((1, HEAD_DIM), lambda b, h, i: (0, 0))],
        out_specs=pl.BlockSpec((tq, HEAD_DIM), lambda b, h, i: (b * nq + i, h)),
        out_shape=jax.ShapeDtypeStruct((t, GROUP), BF16),
        scratch_shapes=[pltpu.VMEM((2 * tq, 1), F32), pltpu.VMEM((2 * tq, 1), F32),
                        pltpu.VMEM((2 * tq, HEAD_DIM), F32)],
        compiler_params=_params("parallel", "parallel", "arbitrary"),
        name="diff_attention",
    )(dq, dk, proj, lam_params, subln_g.reshape(1, HEAD_DIM))


def _tn_dot(a, b):
    return lax.dot_general(a, b, (((0,), (0,)), ((), ())), preferred_element_type=F32)


def _retention_kernel(q_ref, k_ref, v_ref, g_ref, cos_ref, sin_ref, o_ref, state_sc, *, n_chunks):
    @pl.when(pl.program_id(2) == 0)
    def _():
        state_sc[...] = jnp.zeros_like(state_sc)

    h = pl.program_id(1).astype(F32)
    log_gamma = jnp.log(1.0 - jnp.exp2(jnp.full((1, 1), -5.0, F32) - h))
    rowi = lax.broadcasted_iota(jnp.int32, (CHUNK, CHUNK), 0)
    coli = lax.broadcasted_iota(jnp.int32, (CHUNK, CHUNK), 1)
    rel = (rowi - coli).astype(F32)
    intra = jnp.where(rel >= 0, jnp.exp(jnp.maximum(rel, 0.0) * log_gamma), 0.0)
    pos = lax.broadcasted_iota(jnp.int32, (CHUNK, 1), 0).astype(F32)
    q_decay = jnp.exp((pos + 1.0) * log_gamma)
    k_decay = jnp.exp((CHUNK - 1.0 - pos) * log_gamma)
    chunk_decay = jnp.exp(CHUNK * log_gamma)
    k_scale = HEAD_DIM ** -0.5

    state = state_sc[...]
    for c in range(n_chunks):
        sl = slice(c * CHUNK, (c + 1) * CHUNK)
        cos, sin = cos_ref[sl, :], sin_ref[sl, :]
        q = _rotate(q_ref[sl, :].astype(F32), cos, sin).astype(BF16)
        kf = _rotate(k_ref[sl, :].astype(F32), cos, sin) * k_scale
        v = v_ref[sl, :]
        scores = _qk(q, kf.astype(BF16)) * intra
        o = (jnp.dot(scores.astype(BF16), v, preferred_element_type=F32)
             + jnp.dot(q, state.astype(BF16), preferred_element_type=F32) * q_decay)
        state = chunk_decay * state + _tn_dot((kf * k_decay).astype(BF16), v)
        g = g_ref[sl, :].astype(F32)
        o_ref[sl, :] = (g * jax.nn.sigmoid(g) * _rms(o)).astype(o_ref.dtype)
    state_sc[...] = state


def retention(proj, cos, sin, batch):
    t = proj.shape[0]
    s = t // batch
    tb = _tile(s, TILES["recurrent"])
    nb = s // tb

    def col(g):
        return pl.BlockSpec((tb, HEAD_DIM), lambda b, h, i: (b * nb + i, g * HEADS + h))

    tab = pl.BlockSpec((tb, HEAD_DIM), lambda b, h, i: (i, 0))
    return pl.pallas_call(
        functools.partial(_retention_kernel, n_chunks=tb // CHUNK),
        grid=(batch, HEADS, nb),
        in_specs=[col(G_RQ), col(G_RK), col(G_RV), col(G_RG), tab, tab],
        out_specs=pl.BlockSpec((tb, HEAD_DIM), lambda b, h, i: (b * nb + i, h)),
        out_shape=jax.ShapeDtypeStruct((t, GROUP), BF16),
        scratch_shapes=[pltpu.VMEM((HEAD_DIM, HEAD_DIM), F32)],
        compiler_params=_params("parallel", "parallel", "arbitrary"),
        name="retention",
    )(proj, proj, proj, proj, cos, sin)


def _causal_conv_silu(x, prev, w):
    tb = x.shape[0]
    ext = jnp.concatenate([prev, x], axis=0)
    acc = None
    for i in range(CONV_WIDTH):
        off = 8 - (CONV_WIDTH - 1) + i
        term = ext[off:off + tb, :] * w[i:i + 1, :]
        acc = term if acc is None else acc + term
    return acc * jax.nn.sigmoid(acc)


def _mlstm_kernel(q_ref, k_ref, v_ref, og_ref, wq_ref, wk_ref, li_ref, bc_ref, gcol_ref, ng_ref,
                  o_ref, c_sc, n_sc, m_sc, pq_sc, pk_sc, *, n_chunks):
    @pl.when(pl.program_id(2) == 0)
    def _():
        c_sc[...] = jnp.zeros_like(c_sc)
        n_sc[...] = jnp.zeros_like(n_sc)
        m_sc[...] = jnp.zeros_like(m_sc)
        pq_sc[...] = jnp.zeros_like(pq_sc)
        pk_sc[...] = jnp.zeros_like(pk_sc)

    h = pl.program_id(1)
    tb = q_ref.shape[0]
    xq = q_ref[...].astype(F32)
    xk = k_ref[...].astype(F32)
    qf_all = _causal_conv_silu(xq, pq_sc[...], wq_ref[...])
    kf_all = _causal_conv_silu(xk, pk_sc[...], wk_ref[...]) * (HEAD_DIM ** -0.5)
    pq_sc[...] = xq[tb - 8:, :]
    pk_sc[...] = xk[tb - 8:, :]

    lane = lax.broadcasted_iota(jnp.int32, (CHUNK, GATE_ROWS), 1)
    rowi = lax.broadcasted_iota(jnp.int32, (CHUNK, CHUNK), 0)
    coli = lax.broadcasted_iota(jnp.int32, (CHUNK, CHUNK), 1)
    causal = coli <= rowi

    c_mat, n_vec, m = c_sc[...], n_sc[...], m_sc[...]
    for c in range(n_chunks):
        sl = slice(c * CHUNK, (c + 1) * CHUNK)
        qf, kf = qf_all[sl, :], kf_all[sl, :]
        q, k, v = qf.astype(BF16), kf.astype(BF16), v_ref[sl, :]
        li_row, bc_row = li_ref[:, sl], bc_ref[:, sl]
        gc = gcol_ref[sl, :]
        li_col = jnp.sum(jnp.where(lane == HEADS + h, gc, 0.0), axis=1, keepdims=True)
        bc_col = jnp.sum(jnp.where(lane == 2 * HEADS + h, gc, 0.0), axis=1, keepdims=True)
        b_last = bc_row[:, CHUNK - 1:CHUNK]

        dlog = jnp.where(causal, bc_col - bc_row + li_row, -jnp.inf)
        inter = bc_col + m
        m_t = jnp.maximum(inter, jnp.max(dlog, axis=-1, keepdims=True))
        w_intra = jnp.exp(dlog - m_t)
        w_inter = jnp.exp(inter - m_t)
        s_qk = _qk(q, k) * w_intra
        num = (jnp.dot(s_qk.astype(BF16), v, preferred_element_type=F32)
               + w_inter * jnp.dot(q, c_mat.astype(BF16), preferred_element_type=F32))
        den = (jnp.sum(s_qk, axis=-1, keepdims=True)
               + w_inter * jnp.sum(qf * n_vec, axis=-1, keepdims=True))
        h_t = num / jnp.maximum(jnp.abs(den), jnp.exp(-m_t))

        end_log = b_last - bc_col + li_col
        m_new = jnp.maximum(b_last + m, jnp.max(end_log, axis=0, keepdims=True))
        w_end = jnp.exp(end_log - m_new)
        carry_decay = jnp.exp(b_last + m - m_new)
        kw = kf * w_end
        c_mat = carry_decay * c_mat + _tn_dot(kw.astype(BF16), v)
        n_vec = carry_decay * n_vec + jnp.sum(kw, axis=0, keepdims=True)
        m = m_new

        og = og_ref[sl, :].astype(F32)
        o_ref[sl, :] = (jax.nn.sigmoid(og) * (_rms(h_t) * ng_ref[...])).astype(o_ref.dtype)
    c_sc[...], n_sc[...], m_sc[...] = c_mat, n_vec, m


def mlstm(proj, conv_w, gates3, gates_col, norm_g, batch):
    t = proj.shape[0]
    s = t // batch
    tb = _tile(s, TILES["recurrent"])
    nb = s // tb

    def col(g):
        return pl.BlockSpec((tb, HEAD_DIM), lambda b, h, i: (b * nb + i, g * HEADS + h))

    return pl.pallas_call(
        functools.partial(_mlstm_kernel, n_chunks=tb // CHUNK),
        grid=(batch, HEADS, nb),
        in_specs=[col(G_MQ), col(G_MK), col(G_MV), col(G_MO),
                  pl.BlockSpec((CONV_WIDTH, HEAD_DIM), lambda b, h, i: (0, h)),
                  pl.BlockSpec((CONV_WIDTH, HEAD_DIM), lambda b, h, i: (0, HEADS + h)),
                  pl.BlockSpec((None, 1, tb), lambda b, h, i: (HEADS + h, 0, b * nb + i)),
                  pl.BlockSpec((None, 1, tb), lambda b, h, i: (2 * HEADS + h, 0, b * nb + i)),
                  pl.BlockSpec((tb, GATE_ROWS), lambda b, h, i: (b * nb + i, 0)),
                  pl.BlockSpec((1, HEAD_DIM), lambda b, h, i: (0, h))],
        out_specs=pl.BlockSpec((tb, HEAD_DIM), lambda b, h, i: (b * nb + i, h)),
        out_shape=jax.ShapeDtypeStruct((t, GROUP), BF16),
        scratch_shapes=[pltpu.VMEM((HEAD_DIM, HEAD_DIM), F32), pltpu.VMEM((1, HEAD_DIM), F32),
                        pltpu.VMEM((1, 1), F32), pltpu.VMEM((8, HEAD_DIM), F32),
                        pltpu.VMEM((8, HEAD_DIM), F32)],
        compiler_params=_params("parallel", "parallel", "arbitrary"),
        name="mlstm",
    )(proj, proj, proj, proj, conv_w, conv_w, gates3, gates3, gates_col, norm_g.reshape(1, GROUP))


def _head_perm(order):
    return np.concatenate([h * HEAD_DIM + np.asarray(order) for h in range(HEADS)])


def _diff_order():
    q = DIFF_DIM // 2
    return np.concatenate([np.arange(0, q), np.arange(2 * q, 3 * q), np.arange(q, 2 * q), np.arange(3 * q, 4 * q)])


def _ret_order():
    return np.concatenate([np.arange(0, HEAD_DIM, 2), np.arange(1, HEAD_DIM, 2)])


def _prep_w_in(w_in):
    off = 0
    groups = {}
    names = ["fq", "fk", "fv", "ff", "dq", "dk", "dv", "rq", "rk", "rv", "rg", "mq", "mk", "mv", "mo", "mi", "mf"]
    sizes = [GROUP, GROUP, GROUP, HEADS, GROUP, GROUP, GROUP, GROUP, GROUP, GROUP, GROUP,
             GROUP, GROUP, GROUP, GROUP, HEADS, HEADS]
    for n, sz in zip(names, sizes):
        groups[n] = (off, sz)
        off += sz

    def take(n, perm=None):
        o, sz = groups[n]
        w = w_in[:, o:o + sz]
        return w if perm is None else w[:, perm]

    dperm, rperm = _head_perm(_diff_order()), _head_perm(_ret_order())
    wide = jnp.concatenate(
        [take("fq"), take("fk"), take("fv"), take("dq", dperm), take("dk", dperm), take("dv"),
         take("rq", rperm), take("rk", rperm), take("rv"), take("rg"),
         take("mq"), take("mk"), take("mv"), take("mo")], axis=1).astype(BF16)
    d = w_in.shape[0]
    gate_t = jnp.concatenate([take("ff"), take("mi"), take("mf"),
                              jnp.zeros((d, GATE_ROWS - 3 * HEADS), w_in.dtype)], axis=1).T.astype(BF16)
    return wide, gate_t


def _rotary_tables(seq):
    pos = jnp.arange(seq, dtype=F32)[:, None]
    inv_d = 1.0 / (ROPE_THETA ** (jnp.arange(0, DIFF_DIM, 2, dtype=F32) / DIFF_DIM))
    ang_d = pos * inv_d[None, :]
    cos_d = jnp.tile(jnp.cos(ang_d), (1, 4))
    sin_d = jnp.concatenate([-jnp.sin(ang_d)] * 2 + [jnp.sin(ang_d)] * 2, axis=1)
    inv_r = 1.0 / (RET_THETA ** jnp.linspace(0.0, 1.0, HEAD_DIM // 2, dtype=F32))
    ang_r = pos * inv_r[None, :]
    cos_r = jnp.tile(jnp.cos(ang_r), (1, 2))
    sin_r = jnp.concatenate([-jnp.sin(ang_r), jnp.sin(ang_r)], axis=1)
    return cos_d, sin_d, cos_r, sin_r


def kernel(x, norm_gains, ffn_a_gate, ffn_a_up, ffn_a_down, w_in, w_out, fox_fgate_b,
           diff_lq1, diff_lk1, diff_lq2, diff_lk2, diff_subln_g, mlstm_conv_w,
           mlstm_igate_b, mlstm_fgate_b, mlstm_norm_g, ffn_b_gate, ffn_b_up, ffn_b_down):
    batch, seq, d_model = x.shape
    depth = norm_gains.shape[0]
    t = batch * seq
    assert d_model == N_MIXERS * GROUP and seq % CHUNK == 0
    cos_d, sin_d, cos_r, sin_r = _rotary_tables(seq)

    xr = x.reshape(t, d_model)
    xn = rms_norm_rows(xr, norm_gains[0, 0])
    for l in range(depth):
        lam_init = 0.8 - 0.6 * math.exp(-0.3 * l)
        g = norm_gains[l]
        last = l == depth - 1

        hmid = ffn_up(xn, ffn_a_gate[l].astype(BF16), ffn_a_up[l].astype(BF16))
        y = matmul(hmid, ffn_a_down[l].astype(BF16), "ffn_down")
        xr, xn = resid_norm(xr, y, g[1], g[2], 0.5)

        w_wide, w_gate_t = _prep_w_in(w_in[l])
        proj = matmul(xn, w_wide, "in_proj")
        gate_bias = jnp.concatenate([fox_fgate_b[l], mlstm_igate_b[l], mlstm_fgate_b[l],
                                     jnp.zeros((GATE_ROWS - 3 * HEADS,), F32)]).reshape(GATE_ROWS, 1)
        gts = gates(xn, w_gate_t, gate_bias, batch)
        gates3 = gts.reshape(GATE_ROWS, 1, t)
        gates_col = gts.T

        fox_out = fox_attention(proj, gates3, batch)
        dq, dk = rotary_qk(proj, G_DQ, G_DK, cos_d, sin_d, batch, DIFF_DIM ** -0.5)
        lam_params = jnp.stack([diff_lq1[l], diff_lk1[l], diff_lq2[l], diff_lk2[l]]).astype(F32)
        diff_out = diff_attention(dq, dk, proj, lam_params, diff_subln_g[l], batch, lam_init)
        ret_out = retention(proj, cos_r, sin_r, batch)
        mlstm_out = mlstm(proj, mlstm_conv_w[l], gates3, gates_col, mlstm_norm_g[l], batch)

        w_out4 = w_out[l].astype(BF16).reshape(N_MIXERS, GROUP, d_model)
        mix = out_proj([fox_out, diff_out, ret_out, mlstm_out], w_out4)
        xr, xn = resid_norm(xr, mix, g[3], g[4], 1.0)

        hmid = ffn_up(xn, ffn_b_gate[l].astype(BF16), ffn_b_up[l].astype(BF16))
        y = matmul(hmid, ffn_b_down[l].astype(BF16), "ffn_down")
        xr, xn = resid_norm(xr, y, g[5], None if last else norm_gains[l + 1, 0], 0.5)
    return xr.reshape(batch, seq, d_model)
```

```python
import functools
import math

import jax
import jax.numpy as jnp
import numpy as np
from jax import lax
from jax.experimental import pallas as pl
from jax.experimental.pallas import tpu as pltpu

F32 = jnp.float32
BF16 = jnp.bfloat16

N_MIXERS = 4
HEAD_DIM = 128
HEADS = 8
GROUP = HEADS * HEAD_DIM
DIFF_DIM = HEAD_DIM // 2
CHUNK = 128
CONV_WIDTH = 4
NORM_EPS = 1e-6
ROPE_THETA = 10000.0
RET_THETA = 10000.0
N_GROUPS = 14
GATE_ROWS = 32
NEG_BIG = -1e30
ATTN_GROUP = 4

G_FQ, G_FK, G_FV, G_DQ, G_DK, G_DV, G_RQ, G_RK, G_RV, G_RG, G_MQ, G_MK, G_MV, G_MO = range(14)

VMEM_LIMIT_BYTES = 56 * 1024 * 1024

TILES = dict(
    norm_rows=256,
    ffn_up=(1024, 256),
    ffn_down=(512, 512),
    in_proj=(1024, 512),
    out_proj=(1024, 256),
    gates=1024,
    rotary=512,
    attn_q=512,
    recurrent=1024,
)


def _tile(dim, want):
    t = min(dim, want)
    while dim % t:
        t -= 128 if t > 128 else 8
    return t


def _params(*sem):
    return pltpu.CompilerParams(dimension_semantics=sem, vmem_limit_bytes=VMEM_LIMIT_BYTES)


def _rms(x):
    return x * lax.rsqrt(jnp.mean(x * x, axis=-1, keepdims=True) + NORM_EPS)


def _norm_kernel(x_ref, g_ref, o_ref):
    o_ref[...] = (_rms(x_ref[...]) * g_ref[...]).astype(o_ref.dtype)


def rms_norm_rows(x, gain):
    t, d = x.shape
    tm = _tile(t, TILES["norm_rows"])
    return pl.pallas_call(
        _norm_kernel,
        grid=(t // tm,),
        in_specs=[pl.BlockSpec((tm, d), lambda i: (i, 0)),
                  pl.BlockSpec((1, d), lambda i: (0, 0))],
        out_specs=pl.BlockSpec((tm, d), lambda i: (i, 0)),
        out_shape=jax.ShapeDtypeStruct((t, d), BF16),
        compiler_params=_params("parallel"),
        name="rms_norm_rows",
    )(x, gain.reshape(1, d))


def _resid_norm_kernel(x_ref, y_ref, gp_ref, gn_ref, xo_ref, xn_ref, *, coef):
    yn = _rms(y_ref[...].astype(F32)) * gp_ref[...]
    x = x_ref[...] + coef * yn
    xo_ref[...] = x
    xn_ref[...] = (_rms(x) * gn_ref[...]).astype(xn_ref.dtype)


def _resid_kernel(x_ref, y_ref, gp_ref, xo_ref, *, coef):
    yn = _rms(y_ref[...].astype(F32)) * gp_ref[...]
    xo_ref[...] = x_ref[...] + coef * yn


def resid_norm(x, y, g_post, g_next, coef):
    t, d = x.shape
    tm = _tile(t, TILES["norm_rows"])
    row = pl.BlockSpec((tm, d), lambda i: (i, 0))
    vec = pl.BlockSpec((1, d), lambda i: (0, 0))
    if g_next is None:
        return pl.pallas_call(
            functools.partial(_resid_kernel, coef=coef),
            grid=(t // tm,),
            in_specs=[row, row, vec],
            out_specs=row,
            out_shape=jax.ShapeDtypeStruct((t, d), F32),
            compiler_params=_params("parallel"),
            name="resid_last",
        )(x, y, g_post.reshape(1, d)), None
    return pl.pallas_call(
        functools.partial(_resid_norm_kernel, coef=coef),
        grid=(t // tm,),
        in_specs=[row, row, vec, vec],
        out_specs=[row, row],
        out_shape=[jax.ShapeDtypeStruct((t, d), F32), jax.ShapeDtypeStruct((t, d), BF16)],
        compiler_params=_params("parallel"),
        name="resid_norm",
    )(x, y, g_post.reshape(1, d), g_next.reshape(1, d))


def _ffn_up_kernel(x_ref, wg_ref, wu_ref, o_ref):
    x = x_ref[...]
    g = jnp.dot(x, wg_ref[...].astype(x.dtype), preferred_element_type=F32)
    u = jnp.dot(x, wu_ref[...].astype(x.dtype), preferred_element_type=F32)
    o_ref[...] = (g * jax.nn.sigmoid(g) * u).astype(o_ref.dtype)


def ffn_up(xn, w_gate, w_up, layer):
    t, d = xn.shape
    f = w_gate.shape[2]
    bm, bn = _tile(t, TILES["ffn_up"][0]), _tile(f, TILES["ffn_up"][1])
    w_spec = pl.BlockSpec((None, d, bn), lambda i, j: (layer, 0, j))
    return pl.pallas_call(
        _ffn_up_kernel,
        grid=(t // bm, f // bn),
        in_specs=[pl.BlockSpec((bm, d), lambda i, j: (i, 0)), w_spec, w_spec],
        out_specs=pl.BlockSpec((bm, bn), lambda i, j: (i, j)),
        out_shape=jax.ShapeDtypeStruct((t, f), BF16),
        compiler_params=_params("parallel", "arbitrary"),
        name="ffn_up",
    )(xn, w_gate, w_up)


def _matmul_kernel(a_ref, w_ref, o_ref):
    o_ref[...] = jnp.dot(a_ref[...], w_ref[...], preferred_element_type=F32).astype(o_ref.dtype)


def matmul(a, w, name):
    m, k = a.shape
    n = w.shape[1]
    bm, bn = _tile(m, TILES[name][0]), _tile(n, TILES[name][1])
    return pl.pallas_call(
        _matmul_kernel,
        grid=(m // bm, n // bn),
        in_specs=[pl.BlockSpec((bm, k), lambda i, j: (i, 0)),
                  pl.BlockSpec((k, bn), lambda i, j: (0, j))],
        out_specs=pl.BlockSpec((bm, bn), lambda i, j: (i, j)),
        out_shape=jax.ShapeDtypeStruct((m, n), BF16),
        compiler_params=_params("parallel", "arbitrary"),
        name=name,
    )(a, w)


def _out_proj_kernel(a0_ref, a1_ref, a2_ref, a3_ref, w_ref, o_ref):
    acc = jnp.dot(a0_ref[...], w_ref[0], preferred_element_type=F32)
    acc += jnp.dot(a1_ref[...], w_ref[1], preferred_element_type=F32)
    acc += jnp.dot(a2_ref[...], w_ref[2], preferred_element_type=F32)
    acc += jnp.dot(a3_ref[...], w_ref[3], preferred_element_type=F32)
    o_ref[...] = acc.astype(o_ref.dtype)


def out_proj(parts, w_out4):
    t, gw = parts[0].shape
    n = w_out4.shape[2]
    bm, bn = _tile(t, TILES["out_proj"][0]), _tile(n, TILES["out_proj"][1])
    a_spec = pl.BlockSpec((bm, gw), lambda i, j: (i, 0))
    return pl.pallas_call(
        _out_proj_kernel,
        grid=(t // bm, n // bn),
        in_specs=[a_spec, a_spec, a_spec, a_spec,
                  pl.BlockSpec((N_MIXERS, gw, bn), lambda i, j: (0, 0, j))],
        out_specs=pl.BlockSpec((bm, bn), lambda i, j: (i, j)),
        out_shape=jax.ShapeDtypeStruct((t, n), BF16),
        compiler_params=_params("parallel", "arbitrary"),
        name="out_proj",
    )(*parts, w_out4)


def _log_sigmoid(z):
    return jnp.minimum(z, 0.0) - jnp.log(1.0 + jnp.exp(-jnp.abs(z)))


def _chunk_cumsum_lanes(x):
    lane = lax.broadcasted_iota(jnp.int32, x.shape, 1) & (CHUNK - 1)
    s = 1
    while s < CHUNK:
        x = x + jnp.where(lane >= s, pltpu.roll(x, s, 1), 0.0)
        s *= 2
    return x


def _gates_kernel(hn_ref, wt_ref, b_ref, o_ref, carry_sc):
    @pl.when(pl.program_id(1) == 0)
    def _():
        carry_sc[...] = jnp.zeros_like(carry_sc)

    z = lax.dot_general(wt_ref[...], hn_ref[...], (((1,), (1,)), ((), ())),
                        preferred_element_type=F32) + b_ref[...]
    tm = z.shape[1]
    fox = _chunk_cumsum_lanes(_log_sigmoid(z[0:HEADS]))
    carry = carry_sc[...]
    for c in range(tm // CHUNK):
        seg = fox[:, c * CHUNK:(c + 1) * CHUNK] + carry
        o_ref[0:HEADS, c * CHUNK:(c + 1) * CHUNK] = seg
        carry = jnp.broadcast_to(seg[:, CHUNK - 1:CHUNK], carry.shape)
    carry_sc[...] = carry
    o_ref[HEADS:2 * HEADS, :] = z[HEADS:2 * HEADS]
    o_ref[2 * HEADS:3 * HEADS, :] = _chunk_cumsum_lanes(_log_sigmoid(z[2 * HEADS:3 * HEADS]))
    o_ref[3 * HEADS:, :] = jnp.zeros((GATE_ROWS - 3 * HEADS, tm), F32)


def gates(hn, w_gates_t, gate_bias, batch):
    t, d = hn.shape
    s = t // batch
    tm = _tile(s, TILES["gates"])
    ns = s // tm
    return pl.pallas_call(
        _gates_kernel,
        grid=(batch, ns),
        in_specs=[pl.BlockSpec((tm, d), lambda b, i: (b * ns + i, 0)),
                  pl.BlockSpec((GATE_ROWS, d), lambda b, i: (0, 0)),
                  pl.BlockSpec((GATE_ROWS, 1), lambda b, i: (0, 0))],
        out_specs=pl.BlockSpec((GATE_ROWS, tm), lambda b, i: (0, b * ns + i)),
        out_shape=jax.ShapeDtypeStruct((GATE_ROWS, t), F32),
        scratch_shapes=[pltpu.VMEM((HEADS, CHUNK), F32)],
        compiler_params=_params("parallel", "arbitrary"),
        name="gates",
    )(hn, w_gates_t, gate_bias)


def _rotate(x, cos, sin_signed):
    return x * cos + pltpu.roll(x, HEAD_DIM // 2, 1) * sin_signed


def _rotary_kernel(q_ref, k_ref, cos_ref, sin_ref, qo_ref, ko_ref, *, q_scale):
    cos, sin = cos_ref[...], sin_ref[...]
    for h in range(HEADS):
        sl = slice(h * HEAD_DIM, (h + 1) * HEAD_DIM)
        qo_ref[:, sl] = (_rotate(q_ref[:, sl].astype(F32), cos, sin) * q_scale).astype(qo_ref.dtype)
        ko_ref[:, sl] = _rotate(k_ref[:, sl].astype(F32), cos, sin).astype(ko_ref.dtype)


def rotary_qk(proj, gq, gk, cos, sin, batch, q_scale):
    t = proj.shape[0]
    s = t // batch
    tm = _tile(s, TILES["rotary"])
    ns = s // tm
    tab = pl.BlockSpec((tm, HEAD_DIM), lambda i: (i % ns, 0))
    out = pl.BlockSpec((tm, GROUP), lambda i: (i, 0))
    return pl.pallas_call(
        functools.partial(_rotary_kernel, q_scale=q_scale),
        grid=(t // tm,),
        in_specs=[pl.BlockSpec((tm, GROUP), lambda i: (i, gq)),
                  pl.BlockSpec((tm, GROUP), lambda i: (i, gk)), tab, tab],
        out_specs=[out, out],
        out_shape=[jax.ShapeDtypeStruct((t, GROUP), BF16)] * 2,
        compiler_params=_params("parallel"),
        name="rotary_qk",
    )(proj, proj, cos, sin)


def _online_softmax_step(s, v, m_sc, acc_sc):
    m_old = m_sc[...]
    m_new = jnp.maximum(m_old, jnp.max(s, axis=-1, keepdims=True))
    p = jnp.exp(s - jnp.tile(m_new, (1, s.shape[1] // HEAD_DIM)))
    alpha = jnp.exp(m_old - m_new)
    v_ones = jnp.concatenate([v, jnp.ones_like(v)], axis=1)
    acc_sc[...] = (jnp.tile(alpha, (1, 2)) * acc_sc[...]
                   + jnp.dot(p.astype(v.dtype), v_ones, preferred_element_type=F32))
    m_sc[...] = m_new


def _softmax_init(m_sc, acc_sc):
    m_sc[...] = jnp.full_like(m_sc, NEG_BIG)
    acc_sc[...] = jnp.zeros_like(acc_sc)


def _softmax_result(acc_sc):
    acc = acc_sc[...]
    return acc[:, :HEAD_DIM] / acc[:, HEAD_DIM:]


def _qk(q, k):
    return lax.dot_general(q, k, (((1,), (1,)), ((), ())), preferred_element_type=F32)


def _causal_sweep(qi, logits_fn, values_fn, tq, m_sc, acc_sc):
    def full(j):
        _online_softmax_step(logits_fn(j), values_fn(j), m_sc, acc_sc)

    def diagonal():
        s = logits_fn(qi)
        row = lax.broadcasted_iota(jnp.int32, s.shape, 0) & (tq - 1)
        col = lax.broadcasted_iota(jnp.int32, s.shape, 1)
        _online_softmax_step(jnp.where(col <= row, s, NEG_BIG), values_fn(qi), m_sc, acc_sc)

    def group(j, carry):
        for u in range(ATTN_GROUP):
            full(ATTN_GROUP * j + u)
        return carry

    _softmax_init(m_sc, acc_sc)
    lax.fori_loop(0, qi // ATTN_GROUP, group, 0)
    rest = qi % ATTN_GROUP
    for r in range(ATTN_GROUP):
        @pl.when(rest == r)
        def _(r=r):
            for u in range(r):
                full(qi - r + u)
            diagonal()


def _fox_kernel(q_ref, k_ref, v_ref, c_ref, o_ref, m_sc, acc_sc, *, tq, scale):
    qi = pl.program_id(2)
    q = (q_ref[...].astype(F32) * scale).astype(q_ref.dtype)
    c0 = c_ref[:, pl.ds(pl.multiple_of(qi * tq, tq), CHUNK)][:, 0:1]

    def logits(j):
        start = pl.multiple_of(j * tq, tq)
        return _qk(q, k_ref[pl.ds(start, tq), :]) + (c0 - c_ref[:, pl.ds(start, tq)])

    def values(j):
        return v_ref[pl.ds(pl.multiple_of(j * tq, tq), tq), :]

    _causal_sweep(qi, logits, values, tq, m_sc, acc_sc)
    o_ref[...] = _softmax_result(acc_sc).astype(o_ref.dtype)


def fox_attention(proj, cum3, batch):
    t = proj.shape[0]
    s = t // batch
    tq = _tile(s, TILES["attn_q"])
    assert tq & (tq - 1) == 0, "the causal mask takes row mod tq with a bit mask"
    nq = s // tq
    return pl.pallas_call(
        functools.partial(_fox_kernel, tq=tq, scale=HEAD_DIM ** -0.5),
        grid=(batch, HEADS, nq),
        in_specs=[pl.BlockSpec((tq, HEAD_DIM), lambda b, h, i: (b * nq + i, G_FQ * HEADS + h)),
                  pl.BlockSpec((s, HEAD_DIM), lambda b, h, i: (b, G_FK * HEADS + h)),
                  pl.BlockSpec((s, HEAD_DIM), lambda b, h, i: (b, G_FV * HEADS + h)),
                  pl.BlockSpec((None, 1, s), lambda b, h, i: (h, 0, b))],
        out_specs=pl.BlockSpec((tq, HEAD_DIM), lambda b, h, i: (b * nq + i, h)),
        out_shape=jax.ShapeDtypeStruct((t, GROUP), BF16),
        scratch_shapes=[pltpu.VMEM((tq, HEAD_DIM), F32), pltpu.VMEM((tq, 2 * HEAD_DIM), F32)],
        compiler_params=_params("parallel", "parallel", "arbitrary"),
        name="fox_attention",
    )(proj, proj, proj, cum3)


def _diff_kernel(q_ref, k_ref, v_ref, lam_ref, g_ref, o_ref, m_sc, acc_sc, *, tq, lam_init):
    qi = pl.program_id(2)
    q = q_ref[...]
    lane = lax.broadcasted_iota(jnp.int32, q.shape, 1)
    first_map = (lane & (DIFF_DIM // 2)) == 0
    zero = jnp.zeros_like(q)
    q12 = jnp.concatenate([jnp.where(first_map, q, zero), jnp.where(first_map, zero, q)], axis=0)

    def logits(j):
        return _qk(q12, k_ref[pl.ds(pl.multiple_of(j * tq, tq), tq), :])

    def values(j):
        return v_ref[pl.ds(pl.multiple_of(j * tq, tq), tq), :]

    _causal_sweep(qi, logits, values, tq, m_sc, acc_sc)
    o = _softmax_result(acc_sc)
    lp = lam_ref[...]
    lam = (jnp.exp(jnp.sum(lp[0:1] * lp[1:2], axis=-1, keepdims=True))
           - jnp.exp(jnp.sum(lp[2:3] * lp[3:4], axis=-1, keepdims=True)) + lam_init)
    d = o[:tq] - lam * o[tq:]
    o_ref[...] = (_rms(d) * g_ref[...] * (1.0 - lam_init)).astype(o_ref.dtype)


def diff_attention(dq, dk, proj, lam_params, subln_g, batch, lam_init):
    t = proj.shape[0]
    s = t // batch
    tq = _tile(s, TILES["attn_q"])
    assert tq & (tq - 1) == 0, "the causal mask takes row mod tq with a bit mask"
    nq = s // tq
    return pl.pallas_call(
        functools.partial(_diff_kernel, tq=tq, lam_init=lam_init),
        grid=(batch, HEADS, nq),
        in_specs=[pl.BlockSpec((tq, HEAD_DIM), lambda b, h, i: (b * nq + i, h)),
                  pl.BlockSpec((s, HEAD_DIM), lambda b, h, i: (b, h)),
                  pl.BlockSpec((s, HEAD_DIM), lambda b, h, i: (b, G_DV * HEADS + h)),
                  pl.BlockSpec((4, DIFF_DIM), lambda b, h, i: (0, 0)),
                  pl.BlockSpec((1, HEAD_DIM), lambda b, h, i: (0, 0))],
        out_specs=pl.BlockSpec((tq, HEAD_DIM), lambda b, h, i: (b * nq + i, h)),
        out_shape=jax.ShapeDtypeStruct((t, GROUP), BF16),
        scratch_shapes=[pltpu.VMEM((2 * tq, HEAD_DIM), F32), pltpu.VMEM((2 * tq, 2 * HEAD_DIM), F32)],
        compiler_params=_params("parallel", "parallel", "arbitrary"),
        name="diff_attention",
    )(dq, dk, proj, lam_params, subln_g.reshape(1, HEAD_DIM))


def _tn_dot(a, b):
    return lax.dot_general(a, b, (((0,), (0,)), ((), ())), preferred_element_type=F32)


def _retention_kernel(q_ref, k_ref, v_ref, g_ref, cos_ref, sin_ref, o_ref, state_sc, *, n_chunks):
    @pl.when(pl.program_id(2) == 0)
    def _():
        state_sc[...] = jnp.zeros_like(state_sc)

    h = pl.program_id(1).astype(F32)
    log_gamma = jnp.log(1.0 - jnp.exp2(jnp.full((1, 1), -5.0, F32) - h))
    rowi = lax.broadcasted_iota(jnp.int32, (CHUNK, CHUNK), 0)
    coli = lax.broadcasted_iota(jnp.int32, (CHUNK, CHUNK), 1)
    rel = (rowi - coli).astype(F32)
    intra = jnp.where(rel >= 0, jnp.exp(jnp.maximum(rel, 0.0) * log_gamma), 0.0)
    pos = lax.broadcasted_iota(jnp.int32, (CHUNK, 1), 0).astype(F32)
    q_decay = jnp.exp((pos + 1.0) * log_gamma)
    k_decay = jnp.exp((CHUNK - 1.0 - pos) * log_gamma)
    chunk_decay = jnp.exp(CHUNK * log_gamma)
    k_scale = HEAD_DIM ** -0.5

    state = state_sc[...]
    for c in range(n_chunks):
        sl = slice(c * CHUNK, (c + 1) * CHUNK)
        cos, sin = cos_ref[sl, :], sin_ref[sl, :]
        q = _rotate(q_ref[sl, :].astype(F32), cos, sin).astype(BF16)
        kf = _rotate(k_ref[sl, :].astype(F32), cos, sin) * k_scale
        v = v_ref[sl, :]
        scores = _qk(q, kf.astype(BF16)) * intra
        o = (jnp.dot(scores.astype(BF16), v, preferred_element_type=F32)
             + jnp.dot(q, state.astype(BF16), preferred_element_type=F32) * q_decay)
        state = chunk_decay * state + _tn_dot((kf * k_decay).astype(BF16), v)
        g = g_ref[sl, :].astype(F32)
        o_ref[sl, :] = (g * jax.nn.sigmoid(g) * _rms(o)).astype(o_ref.dtype)
    state_sc[...] = state


def retention(proj, cos, sin, batch):
    t = proj.shape[0]
    s = t // batch
    tb = _tile(s, TILES["recurrent"])
    nb = s // tb

    def col(g):
        return pl.BlockSpec((tb, HEAD_DIM), lambda b, h, i: (b * nb + i, g * HEADS + h))

    tab = pl.BlockSpec((tb, HEAD_DIM), lambda b, h, i: (i, 0))
    return pl.pallas_call(
        functools.partial(_retention_kernel, n_chunks=tb // CHUNK),
        grid=(batch, HEADS, nb),
        in_specs=[col(G_RQ), col(G_RK), col(G_RV), col(G_RG), tab, tab],
        out_specs=pl.BlockSpec((tb, HEAD_DIM), lambda b, h, i: (b * nb + i, h)),
        out_shape=jax.ShapeDtypeStruct((t, GROUP), BF16),
        scratch_shapes=[pltpu.VMEM((HEAD_DIM, HEAD_DIM), F32)],
        compiler_params=_params("parallel", "parallel", "arbitrary"),
        name="retention",
    )(proj, proj, proj, proj, cos, sin)


def _causal_conv_silu(x, prev, w):
    tb = x.shape[0]
    ext = jnp.concatenate([prev, x], axis=0)
    acc = None
    for i in range(CONV_WIDTH):
        off = 8 - (CONV_WIDTH - 1) + i
        term = ext[off:off + tb, :] * w[i:i + 1, :]
        acc = term if acc is None else acc + term
    return acc * jax.nn.sigmoid(acc)


def _mlstm_kernel(q_ref, k_ref, v_ref, og_ref, wq_ref, wk_ref, li_ref, bc_ref, gcol_ref, ng_ref,
                  o_ref, c_sc, n_sc, m_sc, pq_sc, pk_sc, *, n_chunks):
    @pl.when(pl.program_id(2) == 0)
    def _():
        c_sc[...] = jnp.zeros_like(c_sc)
        n_sc[...] = jnp.zeros_like(n_sc)
        m_sc[...] = jnp.zeros_like(m_sc)
        pq_sc[...] = jnp.zeros_like(pq_sc)
        pk_sc[...] = jnp.zeros_like(pk_sc)

    h = pl.program_id(1)
    tb = q_ref.shape[0]
    xq = q_ref[...].astype(F32)
    xk = k_ref[...].astype(F32)
    qf_all = _causal_conv_silu(xq, pq_sc[...], wq_ref[...])
    kf_all = _causal_conv_silu(xk, pk_sc[...], wk_ref[...]) * (HEAD_DIM ** -0.5)
    pq_sc[...] = xq[tb - 8:, :]
    pk_sc[...] = xk[tb - 8:, :]

    lane = lax.broadcasted_iota(jnp.int32, (CHUNK, GATE_ROWS), 1)
    rowi = lax.broadcasted_iota(jnp.int32, (CHUNK, CHUNK), 0)
    coli = lax.broadcasted_iota(jnp.int32, (CHUNK, CHUNK), 1)
    causal = coli <= rowi

    c_mat, n_vec, m = c_sc[...], n_sc[...], m_sc[...]
    for c in range(n_chunks):
        sl = slice(c * CHUNK, (c + 1) * CHUNK)
        qf, kf = qf_all[sl, :], kf_all[sl, :]
        q, k, v = qf.astype(BF16), kf.astype(BF16), v_ref[sl, :]
        li_row, bc_row = li_ref[:, sl], bc_ref[:, sl]
        gc = gcol_ref[sl, :]
        li_col = jnp.sum(jnp.where(lane == HEADS + h, gc, 0.0), axis=1, keepdims=True)
        bc_col = jnp.sum(jnp.where(lane == 2 * HEADS + h, gc, 0.0), axis=1, keepdims=True)
        b_last = bc_row[:, CHUNK - 1:CHUNK]

        dlog = jnp.where(causal, bc_col - bc_row + li_row, -jnp.inf)
        inter = bc_col + m
        m_t = jnp.maximum(inter, jnp.max(dlog, axis=-1, keepdims=True))
        w_intra = jnp.exp(dlog - m_t)
        w_inter = jnp.exp(inter - m_t)
        s_qk = _qk(q, k) * w_intra
        num = (jnp.dot(s_qk.astype(BF16), v, preferred_element_type=F32)
               + w_inter * jnp.dot(q, c_mat.astype(BF16), preferred_element_type=F32))
        den = (jnp.sum(s_qk, axis=-1, keepdims=True)
               + w_inter * jnp.sum(qf * n_vec, axis=-1, keepdims=True))
        h_t = num / jnp.maximum(jnp.abs(den), jnp.exp(-m_t))

        end_log = b_last - bc_col + li_col
        m_new = jnp.maximum(b_last + m, jnp.max(end_log, axis=0, keepdims=True))
        w_end = jnp.exp(end_log - m_new)
        carry_decay = jnp.exp(b_last + m - m_new)
        kw = kf * w_end
        c_mat = carry_decay * c_mat + _tn_dot(kw.astype(BF16), v)
        n_vec = carry_decay * n_vec + jnp.sum(kw, axis=0, keepdims=True)
        m = m_new

        og = og_ref[sl, :].astype(F32)
        o_ref[sl, :] = (jax.nn.sigmoid(og) * (_rms(h_t) * ng_ref[...])).astype(o_ref.dtype)
    c_sc[...], n_sc[...], m_sc[...] = c_mat, n_vec, m


def mlstm(proj, conv_w, gates3, gates_col, norm_g, batch):
    t = proj.shape[0]
    s = t // batch
    tb = _tile(s, TILES["recurrent"])
    nb = s // tb

    def col(g):
        return pl.BlockSpec((tb, HEAD_DIM), lambda b, h, i: (b * nb + i, g * HEADS + h))

    return pl.pallas_call(
        functools.partial(_mlstm_kernel, n_chunks=tb // CHUNK),
        grid=(batch, HEADS, nb),
        in_specs=[col(G_MQ), col(G_MK), col(G_MV), col(G_MO),
                  pl.BlockSpec((CONV_WIDTH, HEAD_DIM), lambda b, h, i: (0, h)),
                  pl.BlockSpec((CONV_WIDTH, HEAD_DIM), lambda b, h, i: (0, HEADS + h)),
                  pl.BlockSpec((None, 1, tb), lambda b, h, i: (HEADS + h, 0, b * nb + i)),
                  pl.BlockSpec((None, 1, tb), lambda b, h, i: (2 * HEADS + h, 0, b * nb + i)),
                  pl.BlockSpec((tb, GATE_ROWS), lambda b, h, i: (b * nb + i, 0)),
                  pl.BlockSpec((1, HEAD_DIM), lambda b, h, i: (0, h))],
        out_specs=pl.BlockSpec((tb, HEAD_DIM), lambda b, h, i: (b * nb + i, h)),
        out_shape=jax.ShapeDtypeStruct((t, GROUP), BF16),
        scratch_shapes=[pltpu.VMEM((HEAD_DIM, HEAD_DIM), F32), pltpu.VMEM((1, HEAD_DIM), F32),
                        pltpu.VMEM((1, 1), F32), pltpu.VMEM((8, HEAD_DIM), F32),
                        pltpu.VMEM((8, HEAD_DIM), F32)],
        compiler_params=_params("parallel", "parallel", "arbitrary"),
        name="mlstm",
    )(proj, proj, proj, proj, conv_w, conv_w, gates3, gates3, gates_col, norm_g.reshape(1, GROUP))


def _head_perm(order):
    return np.concatenate([h * HEAD_DIM + np.asarray(order) for h in range(HEADS)])


def _diff_order():
    q = DIFF_DIM // 2
    return np.concatenate([np.arange(0, q), np.arange(2 * q, 3 * q), np.arange(q, 2 * q), np.arange(3 * q, 4 * q)])


def _ret_order():
    return np.concatenate([np.arange(0, HEAD_DIM, 2), np.arange(1, HEAD_DIM, 2)])


def _prep_w_in(w_in):
    off = 0
    groups = {}
    names = ["fq", "fk", "fv", "ff", "dq", "dk", "dv", "rq", "rk", "rv", "rg", "mq", "mk", "mv", "mo", "mi", "mf"]
    sizes = [GROUP, GROUP, GROUP, HEADS, GROUP, GROUP, GROUP, GROUP, GROUP, GROUP, GROUP,
             GROUP, GROUP, GROUP, GROUP, HEADS, HEADS]
    for n, sz in zip(names, sizes):
        groups[n] = (off, sz)
        off += sz

    def take(n, perm=None):
        o, sz = groups[n]
        w = w_in[:, o:o + sz]
        return w if perm is None else w[:, perm]

    dperm, rperm = _head_perm(_diff_order()), _head_perm(_ret_order())
    wide = jnp.concatenate(
        [take("fq"), take("fk"), take("fv"), take("dq", dperm), take("dk", dperm), take("dv"),
         take("rq", rperm), take("rk", rperm), take("rv"), take("rg"),
         take("mq"), take("mk"), take("mv"), take("mo")], axis=1).astype(BF16)
    d = w_in.shape[0]
    gate_t = jnp.concatenate([take("ff"), take("mi"), take("mf"),
                              jnp.zeros((d, GATE_ROWS - 3 * HEADS), w_in.dtype)], axis=1).T.astype(BF16)
    return wide, gate_t


def _rotary_tables(seq):
    pos = jnp.arange(seq, dtype=F32)[:, None]
    inv_d = 1.0 / (ROPE_THETA ** (jnp.arange(0, DIFF_DIM, 2, dtype=F32) / DIFF_DIM))
    ang_d = pos * inv_d[None, :]
    cos_d = jnp.tile(jnp.cos(ang_d), (1, 4))
    sin_d = jnp.concatenate([-jnp.sin(ang_d)] * 2 + [jnp.sin(ang_d)] * 2, axis=1)
    inv_r = 1.0 / (RET_THETA ** jnp.linspace(0.0, 1.0, HEAD_DIM // 2, dtype=F32))
    ang_r = pos * inv_r[None, :]
    cos_r = jnp.tile(jnp.cos(ang_r), (1, 2))
    sin_r = jnp.concatenate([-jnp.sin(ang_r), jnp.sin(ang_r)], axis=1)
    return cos_d, sin_d, cos_r, sin_r


def kernel(x, norm_gains, ffn_a_gate, ffn_a_up, ffn_a_down, w_in, w_out, fox_fgate_b,
           diff_lq1, diff_lk1, diff_lq2, diff_lk2, diff_subln_g, mlstm_conv_w,
           mlstm_igate_b, mlstm_fgate_b, mlstm_norm_g, ffn_b_gate, ffn_b_up, ffn_b_down):
    batch, seq, d_model = x.shape
    depth = norm_gains.shape[0]
    t = batch * seq
    assert d_model == N_MIXERS * GROUP and seq % CHUNK == 0
    cos_d, sin_d, cos_r, sin_r = _rotary_tables(seq)

    xr = x.reshape(t, d_model)
    xn = rms_norm_rows(xr, norm_gains[0, 0])
    for l in range(depth):
        lam_init = 0.8 - 0.6 * math.exp(-0.3 * l)
        g = norm_gains[l]
        last = l == depth - 1

        hmid = ffn_up(xn, ffn_a_gate, ffn_a_up, l)
        y = matmul(hmid, ffn_a_down[l].astype(BF16), "ffn_down")
        xr, xn = resid_norm(xr, y, g[1], g[2], 0.5)

        w_wide, w_gate_t = _prep_w_in(w_in[l])
        proj = matmul(xn, w_wide, "in_proj")
        gate_bias = jnp.concatenate([fox_fgate_b[l], mlstm_igate_b[l], mlstm_fgate_b[l],
                                     jnp.zeros((GATE_ROWS - 3 * HEADS,), F32)]).reshape(GATE_ROWS, 1)
        gts = gates(xn, w_gate_t, gate_bias, batch)
        gates3 = gts.reshape(GATE_ROWS, 1, t)
        gates_col = gts.T

        fox_out = fox_attention(proj, gates3, batch)
        dq, dk = rotary_qk(proj, G_DQ, G_DK, cos_d, sin_d, batch, DIFF_DIM ** -0.5)
        lam_params = jnp.stack([diff_lq1[l], diff_lk1[l], diff_lq2[l], diff_lk2[l]]).astype(F32)
        diff_out = diff_attention(dq, dk, proj, lam_params, diff_subln_g[l], batch, lam_init)
        ret_out = retention(proj, cos_r, sin_r, batch)
        mlstm_out = mlstm(proj, mlstm_conv_w[l], gates3, gates_col, mlstm_norm_g[l], batch)

        w_out4 = w_out[l].astype(BF16).reshape(N_MIXERS, GROUP, d_model)
        mix = out_proj([fox_out, diff_out, ret_out, mlstm_out], w_out4)
        xr, xn = resid_norm(xr, mix, g[3], g[4], 1.0)

        hmid = ffn_up(xn, ffn_b_gate, ffn_b_up, l)
        y = matmul(hmid, ffn_b_down[l].astype(BF16), "ffn_down")
        xr, xn = resid_norm(xr, y, g[5], None if last else norm_gains[l + 1, 0], 0.5)
    return xr.reshape(batch, seq, d_model)
```

```python
import functools
import math

import jax
import jax.numpy as jnp
from jax import lax
from jax.experimental import pallas as pl
from jax.experimental.pallas import tpu as pltpu

F32 = jnp.float32
BF16 = jnp.bfloat16

N_MIXERS = 4
HEAD_DIM = 128
HEADS = 8
GROUP = HEADS * HEAD_DIM
DIFF_DIM = HEAD_DIM // 2
CHUNK = 128
CONV_WIDTH = 4
NORM_EPS = 1e-6
ROPE_THETA = 10000.0
RET_THETA = 10000.0
GATE_ROWS = 32
NEG_BIG = -1e30
LOG2E = 1.4426950408889634
ATTN_GROUP = 4
HEADS_PER_STEP = 2

GA_FQ, GA_FK, GA_FV = range(3)
GB_DQ, GB_DK, GB_DV, GB_RQ, GB_RK, GB_RV, GB_RG, GB_MQ, GB_MK, GB_MV, GB_MO = range(11)

VMEM_LIMIT_BYTES = 56 * 1024 * 1024

TILES = dict(
    norm_rows=256,
    ffn_up=(1024, 256),
    ffn_down=(512, 512),
    in_proj=(1024, 512),
    out_proj=(1024, 512),
    gates=1024,
    rotary=512,
    attn_q=512,
    recurrent=1024,
)


def _tile(dim, want):
    t = min(dim, want)
    while dim % t:
        t -= 128 if t > 128 else 8
    return t


def _params(*sem):
    return pltpu.CompilerParams(dimension_semantics=sem, vmem_limit_bytes=VMEM_LIMIT_BYTES)


def _rms(x):
    return x * lax.rsqrt(jnp.mean(x * x, axis=-1, keepdims=True) + NORM_EPS)


def _norm_kernel(x_ref, g_ref, o_ref):
    o_ref[...] = (_rms(x_ref[...]) * g_ref[...]).astype(o_ref.dtype)


def rms_norm_rows(x, gain):
    t, d = x.shape
    tm = _tile(t, TILES["norm_rows"])
    return pl.pallas_call(
        _norm_kernel,
        grid=(t // tm,),
        in_specs=[pl.BlockSpec((tm, d), lambda i: (i, 0)),
                  pl.BlockSpec((1, d), lambda i: (0, 0))],
        out_specs=pl.BlockSpec((tm, d), lambda i: (i, 0)),
        out_shape=jax.ShapeDtypeStruct((t, d), BF16),
        compiler_params=_params("parallel"),
        name="rms_norm_rows",
    )(x, gain.reshape(1, d))


def _resid_norm_kernel(x_ref, y_ref, gp_ref, gn_ref, xo_ref, xn_ref, *, coef):
    yn = _rms(y_ref[...].astype(F32)) * gp_ref[...]
    x = x_ref[...] + coef * yn
    xo_ref[...] = x
    xn_ref[...] = (_rms(x) * gn_ref[...]).astype(xn_ref.dtype)


def _resid_kernel(x_ref, y_ref, gp_ref, xo_ref, *, coef):
    yn = _rms(y_ref[...].astype(F32)) * gp_ref[...]
    xo_ref[...] = x_ref[...] + coef * yn


def resid_norm(x, y, g_post, g_next, coef):
    t, d = x.shape
    tm = _tile(t, TILES["norm_rows"])
    row = pl.BlockSpec((tm, d), lambda i: (i, 0))
    vec = pl.BlockSpec((1, d), lambda i: (0, 0))
    if g_next is None:
        return pl.pallas_call(
            functools.partial(_resid_kernel, coef=coef),
            grid=(t // tm,),
            in_specs=[row, row, vec],
            out_specs=row,
            out_shape=jax.ShapeDtypeStruct((t, d), F32),
            compiler_params=_params("parallel"),
            name="resid_last",
        )(x, y, g_post.reshape(1, d)), None
    return pl.pallas_call(
        functools.partial(_resid_norm_kernel, coef=coef),
        grid=(t // tm,),
        in_specs=[row, row, vec, vec],
        out_specs=[row, row],
        out_shape=[jax.ShapeDtypeStruct((t, d), F32), jax.ShapeDtypeStruct((t, d), BF16)],
        compiler_params=_params("parallel"),
        name="resid_norm",
    )(x, y, g_post.reshape(1, d), g_next.reshape(1, d))


def _ffn_up_kernel(x_ref, wg_ref, wu_ref, o_ref):
    x = x_ref[...]
    g = jnp.dot(x, wg_ref[...].astype(x.dtype), preferred_element_type=F32)
    u = jnp.dot(x, wu_ref[...].astype(x.dtype), preferred_element_type=F32)
    o_ref[...] = (g * jax.nn.sigmoid(g) * u).astype(o_ref.dtype)


def ffn_up(xn, w_gate, w_up, layer):
    t, d = xn.shape
    f = w_gate.shape[2]
    bm, bn = _tile(t, TILES["ffn_up"][0]), _tile(f, TILES["ffn_up"][1])
    w_spec = pl.BlockSpec((None, d, bn), lambda i, j: (layer, 0, j))
    return pl.pallas_call(
        _ffn_up_kernel,
        grid=(t // bm, f // bn),
        in_specs=[pl.BlockSpec((bm, d), lambda i, j: (i, 0)), w_spec, w_spec],
        out_specs=pl.BlockSpec((bm, bn), lambda i, j: (i, j)),
        out_shape=jax.ShapeDtypeStruct((t, f), BF16),
        compiler_params=_params("parallel", "arbitrary"),
        name="ffn_up",
    )(xn, w_gate, w_up)


def _matmul_kernel(a_ref, w_ref, o_ref):
    a = a_ref[...]
    o_ref[...] = jnp.dot(a, w_ref[...].astype(a.dtype), preferred_element_type=F32).astype(o_ref.dtype)


def matmul(a, w, tiles, name, layer=None, n=None):
    m, k = a.shape
    n = w.shape[-1] if n is None else n
    bm, bn = _tile(m, TILES[tiles][0]), _tile(n, TILES[tiles][1])
    if layer is None:
        w_spec = pl.BlockSpec((k, bn), lambda i, j: (0, j))
    else:
        w_spec = pl.BlockSpec((None, k, bn), lambda i, j: (layer, 0, j))
    return pl.pallas_call(
        _matmul_kernel,
        grid=(m // bm, n // bn),
        in_specs=[pl.BlockSpec((bm, k), lambda i, j: (i, 0)), w_spec],
        out_specs=pl.BlockSpec((bm, bn), lambda i, j: (i, j)),
        out_shape=jax.ShapeDtypeStruct((m, n), BF16),
        compiler_params=_params("parallel", "arbitrary"),
        name=name,
    )(a, w)


def _out_proj_kernel(a0_ref, a1_ref, a2_ref, a3_ref, w_ref, o_ref):
    acc = jnp.dot(a0_ref[...], w_ref[0], preferred_element_type=F32)
    acc += jnp.dot(a1_ref[...], w_ref[1], preferred_element_type=F32)
    acc += jnp.dot(a2_ref[...], w_ref[2], preferred_element_type=F32)
    acc += jnp.dot(a3_ref[...], w_ref[3], preferred_element_type=F32)
    o_ref[...] = acc.astype(o_ref.dtype)


def out_proj(parts, w_out4):
    t, gw = parts[0].shape
    n = w_out4.shape[2]
    bm, bn = _tile(t, TILES["out_proj"][0]), _tile(n, TILES["out_proj"][1])
    a_spec = pl.BlockSpec((bm, gw), lambda i, j: (i, 0))
    return pl.pallas_call(
        _out_proj_kernel,
        grid=(t // bm, n // bn),
        in_specs=[a_spec, a_spec, a_spec, a_spec,
                  pl.BlockSpec((N_MIXERS, gw, bn), lambda i, j: (0, 0, j))],
        out_specs=pl.BlockSpec((bm, bn), lambda i, j: (i, j)),
        out_shape=jax.ShapeDtypeStruct((t, n), BF16),
        compiler_params=_params("parallel", "arbitrary"),
        name="out_proj",
    )(*parts, w_out4)


def _log_sigmoid(z):
    return jnp.minimum(z, 0.0) - jnp.log(1.0 + jnp.exp(-jnp.abs(z)))


def _chunk_cumsum_lanes(x):
    lane = lax.broadcasted_iota(jnp.int32, x.shape, 1) & (CHUNK - 1)
    s = 1
    while s < CHUNK:
        x = x + jnp.where(lane >= s, pltpu.roll(x, s, 1), 0.0)
        s *= 2
    return x


def _gates_kernel(hn_ref, wt_ref, b_ref, o_ref, carry_sc):
    @pl.when(pl.program_id(1) == 0)
    def _():
        carry_sc[...] = jnp.zeros_like(carry_sc)

    z = lax.dot_general(wt_ref[...], hn_ref[...], (((1,), (1,)), ((), ())),
                        preferred_element_type=F32) + b_ref[...]
    tm = z.shape[1]
    fox = _chunk_cumsum_lanes(_log_sigmoid(z[0:HEADS]))
    carry = carry_sc[...]
    for c in range(tm // CHUNK):
        seg = fox[:, c * CHUNK:(c + 1) * CHUNK] + carry
        o_ref[0:HEADS, c * CHUNK:(c + 1) * CHUNK] = seg
        carry = jnp.broadcast_to(seg[:, CHUNK - 1:CHUNK], carry.shape)
    carry_sc[...] = carry
    o_ref[HEADS:2 * HEADS, :] = z[HEADS:2 * HEADS]
    o_ref[2 * HEADS:3 * HEADS, :] = _chunk_cumsum_lanes(_log_sigmoid(z[2 * HEADS:3 * HEADS]))
    o_ref[3 * HEADS:, :] = jnp.zeros((GATE_ROWS - 3 * HEADS, tm), F32)


def gates(hn, w_gates_t, gate_bias, batch):
    t, d = hn.shape
    s = t // batch
    tm = _tile(s, TILES["gates"])
    ns = s // tm
    return pl.pallas_call(
        _gates_kernel,
        grid=(batch, ns),
        in_specs=[pl.BlockSpec((tm, d), lambda b, i: (b * ns + i, 0)),
                  pl.BlockSpec((GATE_ROWS, d), lambda b, i: (0, 0)),
                  pl.BlockSpec((GATE_ROWS, 1), lambda b, i: (0, 0))],
        out_specs=pl.BlockSpec((GATE_ROWS, tm), lambda b, i: (0, b * ns + i)),
        out_shape=jax.ShapeDtypeStruct((GATE_ROWS, t), F32),
        scratch_shapes=[pltpu.VMEM((HEADS, CHUNK), F32)],
        compiler_params=_params("parallel", "arbitrary"),
        name="gates",
    )(hn, w_gates_t, gate_bias)


DIFF_PAIR = DIFF_DIM // 2
RET_PAIR = 1


def _rotate(x, cos, sin_signed, pair):
    lane = lax.broadcasted_iota(jnp.int32, x.shape, 1)
    partner = jnp.where((lane & pair) != 0, pltpu.roll(x, pair, 1), pltpu.roll(x, HEAD_DIM - pair, 1))
    return x * cos + partner * sin_signed


def _rotary_kernel(q_ref, k_ref, cos_ref, sin_ref, qo_ref, ko_ref, *, q_scale):
    cos, sin = cos_ref[...], sin_ref[...]
    for h in range(HEADS):
        sl = slice(h * HEAD_DIM, (h + 1) * HEAD_DIM)
        qo_ref[:, sl] = (_rotate(q_ref[:, sl].astype(F32), cos, sin, DIFF_PAIR) * q_scale).astype(qo_ref.dtype)
        ko_ref[:, sl] = _rotate(k_ref[:, sl].astype(F32), cos, sin, DIFF_PAIR).astype(ko_ref.dtype)


def rotary_qk(proj, gq, gk, cos, sin, batch, q_scale):
    t = proj.shape[0]
    s = t // batch
    tm = _tile(s, TILES["rotary"])
    ns = s // tm
    tab = pl.BlockSpec((tm, HEAD_DIM), lambda i: (i % ns, 0))
    out = pl.BlockSpec((tm, GROUP), lambda i: (i, 0))
    return pl.pallas_call(
        functools.partial(_rotary_kernel, q_scale=q_scale),
        grid=(t // tm,),
        in_specs=[pl.BlockSpec((tm, GROUP), lambda i: (i, gq)),
                  pl.BlockSpec((tm, GROUP), lambda i: (i, gk)), tab, tab],
        out_specs=[out, out],
        out_shape=[jax.ShapeDtypeStruct((t, GROUP), BF16)] * 2,
        compiler_params=_params("parallel"),
        name="rotary_qk",
    )(proj, proj, cos, sin)


def _online_softmax_step(s, v, m_sc, acc_sc):
    m_old = m_sc[...]
    m_new = jnp.maximum(m_old, jnp.max(s, axis=-1, keepdims=True))
    p = jnp.exp2(s - jnp.tile(m_new, (1, s.shape[1] // HEAD_DIM)))
    alpha = jnp.exp2(m_old - m_new)
    v_ones = jnp.concatenate([v, jnp.ones_like(v)], axis=1)
    acc_sc[...] = (jnp.tile(alpha, (1, 2)) * acc_sc[...]
                   + jnp.dot(p.astype(v.dtype), v_ones, preferred_element_type=F32))
    m_sc[...] = m_new


def _softmax_init(m_sc, acc_sc):
    m_sc[...] = jnp.full_like(m_sc, NEG_BIG)
    acc_sc[...] = jnp.zeros_like(acc_sc)


def _softmax_result(acc_sc):
    acc = acc_sc[...]
    return acc[:, :HEAD_DIM] / acc[:, HEAD_DIM:]


def _qk(q, k):
    return lax.dot_general(q, k, (((1,), (1,)), ((), ())), preferred_element_type=F32)


def _causal_sweep(qi, logits_fn, values_fn, tq, m_sc, acc_sc):
    def full(j):
        _online_softmax_step(logits_fn(j), values_fn(j), m_sc, acc_sc)

    def diagonal():
        s = logits_fn(qi)
        row = lax.broadcasted_iota(jnp.int32, s.shape, 0) & (tq - 1)
        col = lax.broadcasted_iota(jnp.int32, s.shape, 1)
        _online_softmax_step(jnp.where(col <= row, s, NEG_BIG), values_fn(qi), m_sc, acc_sc)

    def group(j, carry):
        for u in range(ATTN_GROUP):
            full(ATTN_GROUP * j + u)
        return carry

    _softmax_init(m_sc, acc_sc)
    lax.fori_loop(0, qi // ATTN_GROUP, group, 0)
    rest = qi % ATTN_GROUP
    for r in range(ATTN_GROUP):
        @pl.when(rest == r)
        def _(r=r):
            for u in range(r):
                full(qi - r + u)
            diagonal()


def _fox_kernel(q_ref, k_ref, v_ref, c_ref, o_ref, m_sc, acc_sc, *, tq, scale):
    qi = pl.program_id(2)
    q = (q_ref[...].astype(F32) * scale).astype(q_ref.dtype)
    c0 = c_ref[:, pl.ds(pl.multiple_of(qi * tq, tq), CHUNK)][:, 0:1]

    def logits(j):
        start = pl.multiple_of(j * tq, tq)
        return _qk(q, k_ref[pl.ds(start, tq), :]) + (c0 - c_ref[:, pl.ds(start, tq)]) * LOG2E

    def values(j):
        return v_ref[pl.ds(pl.multiple_of(j * tq, tq), tq), :]

    _causal_sweep(qi, logits, values, tq, m_sc, acc_sc)
    o_ref[...] = _softmax_result(acc_sc).astype(o_ref.dtype)


def fox_attention(proj, cum3, batch):
    t = proj.shape[0]
    s = t // batch
    tq = _tile(s, TILES["attn_q"])
    assert tq & (tq - 1) == 0, "the causal mask takes row mod tq with a bit mask"
    nq = s // tq
    return pl.pallas_call(
        functools.partial(_fox_kernel, tq=tq, scale=HEAD_DIM ** -0.5 * LOG2E),
        grid=(batch, HEADS, nq),
        in_specs=[pl.BlockSpec((tq, HEAD_DIM), lambda b, h, i: (b * nq + i, GA_FQ * HEADS + h)),
                  pl.BlockSpec((s, HEAD_DIM), lambda b, h, i: (b, GA_FK * HEADS + h)),
                  pl.BlockSpec((s, HEAD_DIM), lambda b, h, i: (b, GA_FV * HEADS + h)),
                  pl.BlockSpec((None, 1, s), lambda b, h, i: (h, 0, b))],
        out_specs=pl.BlockSpec((tq, HEAD_DIM), lambda b, h, i: (b * nq + i, h)),
        out_shape=jax.ShapeDtypeStruct((t, GROUP), BF16),
        scratch_shapes=[pltpu.VMEM((tq, HEAD_DIM), F32), pltpu.VMEM((tq, 2 * HEAD_DIM), F32)],
        compiler_params=_params("parallel", "parallel", "arbitrary"),
        name="fox_attention",
    )(proj, proj, proj, cum3)


def _diff_kernel(q_ref, k_ref, v_ref, lam_ref, g_ref, o_ref, m_sc, acc_sc, *, tq, lam_init):
    qi = pl.program_id(2)
    q = q_ref[...]
    lane = lax.broadcasted_iota(jnp.int32, q.shape, 1)
    first_map = lane < DIFF_DIM
    zero = jnp.zeros_like(q)
    q12 = jnp.concatenate([jnp.where(first_map, q, zero), jnp.where(first_map, zero, q)], axis=0)

    def logits(j):
        return _qk(q12, k_ref[pl.ds(pl.multiple_of(j * tq, tq), tq), :])

    def values(j):
        return v_ref[pl.ds(pl.multiple_of(j * tq, tq), tq), :]

    _causal_sweep(qi, logits, values, tq, m_sc, acc_sc)
    o = _softmax_result(acc_sc)
    lp = lam_ref[...]
    lam = (jnp.exp(jnp.sum(lp[0:1] * lp[1:2], axis=-1, keepdims=True))
           - jnp.exp(jnp.sum(lp[2:3] * lp[3:4], axis=-1, keepdims=True)) + lam_init)
    d = o[:tq] - lam * o[tq:]
    o_ref[...] = (_rms(d) * g_ref[...] * (1.0 - lam_init)).astype(o_ref.dtype)


def diff_attention(dq, dk, proj, lam_params, subln_g, batch, lam_init):
    t = proj.shape[0]
    s = t // batch
    tq = _tile(s, TILES["attn_q"])
    assert tq & (tq - 1) == 0, "the causal mask takes row mod tq with a bit mask"
    nq = s // tq
    return pl.pallas_call(
        functools.partial(_diff_kernel, tq=tq, lam_init=lam_init),
        grid=(batch, HEADS, nq),
        in_specs=[pl.BlockSpec((tq, HEAD_DIM), lambda b, h, i: (b * nq + i, h)),
                  pl.BlockSpec((s, HEAD_DIM), lambda b, h, i: (b, h)),
                  pl.BlockSpec((s, HEAD_DIM), lambda b, h, i: (b, GB_DV * HEADS + h)),
                  pl.BlockSpec((4, DIFF_DIM), lambda b, h, i: (0, 0)),
                  pl.BlockSpec((1, HEAD_DIM), lambda b, h, i: (0, 0))],
        out_specs=pl.BlockSpec((tq, HEAD_DIM), lambda b, h, i: (b * nq + i, h)),
        out_shape=jax.ShapeDtypeStruct((t, GROUP), BF16),
        scratch_shapes=[pltpu.VMEM((2 * tq, HEAD_DIM), F32), pltpu.VMEM((2 * tq, 2 * HEAD_DIM), F32)],
        compiler_params=_params("parallel", "parallel", "arbitrary"),
        name="diff_attention",
    )(dq, dk, proj, lam_params, subln_g.reshape(1, HEAD_DIM))


def _tn_dot(a, b):
    return lax.dot_general(a, b, (((0,), (0,)), ((), ())), preferred_element_type=F32)


def _retention_kernel(q_ref, k_ref, v_ref, g_ref, cos_ref, sin_ref, o_ref, state_sc, *, n_chunks):
    @pl.when(pl.program_id(2) == 0)
    def _():
        state_sc[...] = jnp.zeros_like(state_sc)

    rowi = lax.broadcasted_iota(jnp.int32, (CHUNK, CHUNK), 0)
    coli = lax.broadcasted_iota(jnp.int32, (CHUNK, CHUNK), 1)
    rel = (rowi - coli).astype(F32)
    pos = lax.broadcasted_iota(jnp.int32, (CHUNK, 1), 0).astype(F32)
    k_scale = HEAD_DIM ** -0.5

    decays = []
    for hh in range(HEADS_PER_STEP):
        h = (pl.program_id(1) * HEADS_PER_STEP + hh).astype(F32)
        log_gamma = jnp.log(1.0 - jnp.exp2(jnp.full((1, 1), -5.0, F32) - h))
        decays.append((jnp.where(rel >= 0, jnp.exp(jnp.maximum(rel, 0.0) * log_gamma), 0.0),
                       jnp.exp((pos + 1.0) * log_gamma),
                       jnp.exp((CHUNK - 1.0 - pos) * log_gamma),
                       jnp.exp(CHUNK * log_gamma)))

    states = [state_sc[hh] for hh in range(HEADS_PER_STEP)]
    for c in range(n_chunks):
        rows = slice(c * CHUNK, (c + 1) * CHUNK)
        cos, sin = cos_ref[rows, :], sin_ref[rows, :]
        for hh in range(HEADS_PER_STEP):
            intra, q_decay, k_decay, chunk_decay = decays[hh]
            cols = slice(hh * HEAD_DIM, (hh + 1) * HEAD_DIM)
            q = _rotate(q_ref[rows, cols].astype(F32), cos, sin, RET_PAIR).astype(BF16)
            kf = _rotate(k_ref[rows, cols].astype(F32), cos, sin, RET_PAIR) * k_scale
            v = v_ref[rows, cols]
            scores = _qk(q, kf.astype(BF16)) * intra
            o = (jnp.dot(scores.astype(BF16), v, preferred_element_type=F32)
                 + jnp.dot(q, states[hh].astype(BF16), preferred_element_type=F32) * q_decay)
            states[hh] = chunk_decay * states[hh] + _tn_dot((kf * k_decay).astype(BF16), v)
            g = g_ref[rows, cols].astype(F32)
            o_ref[rows, cols] = (g * jax.nn.sigmoid(g) * _rms(o)).astype(o_ref.dtype)
    for hh in range(HEADS_PER_STEP):
        state_sc[hh] = states[hh]


def _head_cols(tb, nb, group):
    width = HEADS_PER_STEP * HEAD_DIM
    return pl.BlockSpec((tb, width), lambda b, hp, i: (b * nb + i, group * (GROUP // width) + hp))


def retention(proj, cos, sin, batch):
    t = proj.shape[0]
    s = t // batch
    tb = _tile(s, TILES["recurrent"])
    nb = s // tb
    tab = pl.BlockSpec((tb, HEAD_DIM), lambda b, hp, i: (i, 0))
    return pl.pallas_call(
        functools.partial(_retention_kernel, n_chunks=tb // CHUNK),
        grid=(batch, HEADS // HEADS_PER_STEP, nb),
        in_specs=[_head_cols(tb, nb, GB_RQ), _head_cols(tb, nb, GB_RK), _head_cols(tb, nb, GB_RV),
                  _head_cols(tb, nb, GB_RG), tab, tab],
        out_specs=_head_cols(tb, nb, 0),
        out_shape=jax.ShapeDtypeStruct((t, GROUP), BF16),
        scratch_shapes=[pltpu.VMEM((HEADS_PER_STEP, HEAD_DIM, HEAD_DIM), F32)],
        compiler_params=_params("parallel", "parallel", "arbitrary"),
        name="retention",
    )(proj, proj, proj, proj, cos, sin)


def _causal_conv_silu(x, prev, w):
    tb = x.shape[0]
    ext = jnp.concatenate([prev, x], axis=0)
    acc = None
    for i in range(CONV_WIDTH):
        off = 8 - (CONV_WIDTH - 1) + i
        term = ext[off:off + tb, :] * w[i:i + 1, :]
        acc = term if acc is None else acc + term
    return acc * jax.nn.sigmoid(acc)


def _mlstm_kernel(q_ref, k_ref, v_ref, og_ref, wq_ref, wk_ref, li_ref, bc_ref, gcol_ref, ng_ref,
                  o_ref, c_sc, n_sc, m_sc, pq_sc, pk_sc, *, n_chunks):
    @pl.when(pl.program_id(2) == 0)
    def _():
        c_sc[...] = jnp.zeros_like(c_sc)
        n_sc[...] = jnp.zeros_like(n_sc)
        m_sc[...] = jnp.zeros_like(m_sc)
        pq_sc[...] = jnp.zeros_like(pq_sc)
        pk_sc[...] = jnp.zeros_like(pk_sc)

    tb = q_ref.shape[0]
    xq = q_ref[...].astype(F32)
    xk = k_ref[...].astype(F32)
    qf_all = _causal_conv_silu(xq, pq_sc[...], wq_ref[...])
    kf_all = _causal_conv_silu(xk, pk_sc[...], wk_ref[...]) * (HEAD_DIM ** -0.5)
    pq_sc[...] = xq[tb - 8:, :]
    pk_sc[...] = xk[tb - 8:, :]

    lane = lax.broadcasted_iota(jnp.int32, (CHUNK, GATE_ROWS), 1)
    rowi = lax.broadcasted_iota(jnp.int32, (CHUNK, CHUNK), 0)
    coli = lax.broadcasted_iota(jnp.int32, (CHUNK, CHUNK), 1)
    causal = coli <= rowi

    carries = [(c_sc[hh], n_sc[hh], m_sc[hh]) for hh in range(HEADS_PER_STEP)]
    for c in range(n_chunks):
        rows = slice(c * CHUNK, (c + 1) * CHUNK)
        gc = gcol_ref[rows, :]
        for hh in range(HEADS_PER_STEP):
            h = pl.program_id(1) * HEADS_PER_STEP + hh
            c_mat, n_vec, m = carries[hh]
            cols = slice(hh * HEAD_DIM, (hh + 1) * HEAD_DIM)
            qf, kf = qf_all[rows, cols], kf_all[rows, cols]
            q, k, v = qf.astype(BF16), kf.astype(BF16), v_ref[rows, cols]
            li_row, bc_row = li_ref[hh, :, rows], bc_ref[hh, :, rows]
            li_col = jnp.sum(jnp.where(lane == HEADS + h, gc, 0.0), axis=1, keepdims=True)
            bc_col = jnp.sum(jnp.where(lane == 2 * HEADS + h, gc, 0.0), axis=1, keepdims=True)
            b_last = bc_row[:, CHUNK - 1:CHUNK]

            dlog = jnp.where(causal, bc_col - bc_row + li_row, -jnp.inf)
            inter = bc_col + m
            m_t = jnp.maximum(inter, jnp.max(dlog, axis=-1, keepdims=True))
            w_intra = jnp.exp(dlog - m_t)
            w_inter = jnp.exp(inter - m_t)
            s_qk = _qk(q, k) * w_intra
            num = (jnp.dot(s_qk.astype(BF16), v, preferred_element_type=F32)
                   + w_inter * jnp.dot(q, c_mat.astype(BF16), preferred_element_type=F32))
            den = (jnp.sum(s_qk, axis=-1, keepdims=True)
                   + w_inter * jnp.sum(qf * n_vec, axis=-1, keepdims=True))
            h_t = num / jnp.maximum(jnp.abs(den), jnp.exp(-m_t))

            end_log = b_last - bc_col + li_col
            m_new = jnp.maximum(b_last + m, jnp.max(end_log, axis=0, keepdims=True))
            w_end = jnp.exp(end_log - m_new)
            carry_decay = jnp.exp(b_last + m - m_new)
            kw = kf * w_end
            carries[hh] = (carry_decay * c_mat + _tn_dot(kw.astype(BF16), v),
                           carry_decay * n_vec + jnp.sum(kw, axis=0, keepdims=True),
                           m_new)

            og = og_ref[rows, cols].astype(F32)
            o_ref[rows, cols] = (jax.nn.sigmoid(og) * (_rms(h_t) * ng_ref[:, cols])).astype(o_ref.dtype)
    for hh in range(HEADS_PER_STEP):
        c_sc[hh], n_sc[hh], m_sc[hh] = carries[hh]


def mlstm(proj, conv_w, gates3, gates_col, norm_g, batch):
    t = proj.shape[0]
    s = t // batch
    tb = _tile(s, TILES["recurrent"])
    nb = s // tb
    hps = HEADS_PER_STEP
    width = hps * HEAD_DIM
    n_hp = HEADS // hps
    return pl.pallas_call(
        functools.partial(_mlstm_kernel, n_chunks=tb // CHUNK),
        grid=(batch, n_hp, nb),
        in_specs=[_head_cols(tb, nb, GB_MQ), _head_cols(tb, nb, GB_MK), _head_cols(tb, nb, GB_MV),
                  _head_cols(tb, nb, GB_MO),
                  pl.BlockSpec((CONV_WIDTH, width), lambda b, hp, i: (0, hp)),
                  pl.BlockSpec((CONV_WIDTH, width), lambda b, hp, i: (0, n_hp + hp)),
                  pl.BlockSpec((hps, 1, tb), lambda b, hp, i: (n_hp + hp, 0, b * nb + i)),
                  pl.BlockSpec((hps, 1, tb), lambda b, hp, i: (2 * n_hp + hp, 0, b * nb + i)),
                  pl.BlockSpec((tb, GATE_ROWS), lambda b, hp, i: (b * nb + i, 0)),
                  pl.BlockSpec((1, width), lambda b, hp, i: (0, hp))],
        out_specs=_head_cols(tb, nb, 0),
        out_shape=jax.ShapeDtypeStruct((t, GROUP), BF16),
        scratch_shapes=[pltpu.VMEM((hps, HEAD_DIM, HEAD_DIM), F32), pltpu.VMEM((hps, 1, HEAD_DIM), F32),
                        pltpu.VMEM((hps, 1, 1), F32), pltpu.VMEM((8, width), F32),
                        pltpu.VMEM((8, width), F32)],
        compiler_params=_params("parallel", "parallel", "arbitrary"),
        name="mlstm",
    )(proj, proj, proj, proj, conv_w, conv_w, gates3, gates3, gates_col, norm_g.reshape(1, GROUP))


W_IN_A = N_MIXERS - 1
W_IN_B = 11
W_IN_B_START = W_IN_A * GROUP + HEADS
W_IN_GATES = W_IN_B_START + W_IN_B * GROUP


def _prep_w_in(w_in_l):
    wide_b = w_in_l[:, W_IN_B_START:W_IN_GATES].astype(BF16)
    d = w_in_l.shape[0]
    gate_t = jnp.concatenate([w_in_l[:, W_IN_A * GROUP:W_IN_B_START], w_in_l[:, W_IN_GATES:],
                              jnp.zeros((d, GATE_ROWS - 3 * HEADS), w_in_l.dtype)], axis=1).T.astype(BF16)
    return wide_b, gate_t


def _rotary_tables(seq):
    pos = jnp.arange(seq, dtype=F32)[:, None]
    inv_d = 1.0 / (ROPE_THETA ** (jnp.arange(0, DIFF_DIM, 2, dtype=F32) / DIFF_DIM))
    ang_d = pos * inv_d[None, :]
    cos_d = jnp.tile(jnp.cos(ang_d), (1, 4))
    sin_d = jnp.tile(jnp.concatenate([-jnp.sin(ang_d), jnp.sin(ang_d)], axis=1), (1, 2))
    inv_r = 1.0 / (RET_THETA ** jnp.linspace(0.0, 1.0, HEAD_DIM // 2, dtype=F32))
    ang_r = pos * inv_r[None, :]
    cos_r = jnp.repeat(jnp.cos(ang_r), 2, axis=1)
    sin_r = jnp.stack([-jnp.sin(ang_r), jnp.sin(ang_r)], axis=-1).reshape(seq, HEAD_DIM)
    return cos_d, sin_d, cos_r, sin_r


def kernel(x, norm_gains, ffn_a_gate, ffn_a_up, ffn_a_down, w_in, w_out, fox_fgate_b,
           diff_lq1, diff_lk1, diff_lq2, diff_lk2, diff_subln_g, mlstm_conv_w,
           mlstm_igate_b, mlstm_fgate_b, mlstm_norm_g, ffn_b_gate, ffn_b_up, ffn_b_down):
    batch, seq, d_model = x.shape
    depth = norm_gains.shape[0]
    t = batch * seq
    assert d_model == N_MIXERS * GROUP and seq % CHUNK == 0
    cos_d, sin_d, cos_r, sin_r = _rotary_tables(seq)

    xr = x.reshape(t, d_model)
    xn = rms_norm_rows(xr, norm_gains[0, 0])
    for l in range(depth):
        lam_init = 0.8 - 0.6 * math.exp(-0.3 * l)
        g = norm_gains[l]
        last = l == depth - 1

        hmid = ffn_up(xn, ffn_a_gate, ffn_a_up, l)
        y = matmul(hmid, ffn_a_down[l].astype(BF16), "ffn_down", "ffn_down")
        xr, xn = resid_norm(xr, y, g[1], g[2], 0.5)

        w_wide_b, w_gate_t = _prep_w_in(w_in[l])
        proj_a = matmul(xn, w_in, "in_proj", "in_proj_fox", layer=l, n=W_IN_A * GROUP)
        proj_b = matmul(xn, w_wide_b, "in_proj", "in_proj")
        gate_bias = jnp.concatenate([fox_fgate_b[l], mlstm_igate_b[l], mlstm_fgate_b[l],
                                     jnp.zeros((GATE_ROWS - 3 * HEADS,), F32)]).reshape(GATE_ROWS, 1)
        gts = gates(xn, w_gate_t, gate_bias, batch)
        gates3 = gts.reshape(GATE_ROWS, 1, t)
        gates_col = gts.T

        fox_out = fox_attention(proj_a, gates3, batch)
        dq, dk = rotary_qk(proj_b, GB_DQ, GB_DK, cos_d, sin_d, batch, DIFF_DIM ** -0.5 * LOG2E)
        lam_params = jnp.stack([diff_lq1[l], diff_lk1[l], diff_lq2[l], diff_lk2[l]]).astype(F32)
        diff_out = diff_attention(dq, dk, proj_b, lam_params, diff_subln_g[l], batch, lam_init)
        ret_out = retention(proj_b, cos_r, sin_r, batch)
        mlstm_out = mlstm(proj_b, mlstm_conv_w[l], gates3, gates_col, mlstm_norm_g[l], batch)

        w_out4 = w_out[l].astype(BF16).reshape(N_MIXERS, GROUP, d_model)
        mix = out_proj([fox_out, diff_out, ret_out, mlstm_out], w_out4)
        xr, xn = resid_norm(xr, mix, g[3], g[4], 1.0)

        hmid = ffn_up(xn, ffn_b_gate, ffn_b_up, l)
        y = matmul(hmid, ffn_b_down[l].astype(BF16), "ffn_down", "ffn_down")
        xr, xn = resid_norm(xr, y, g[5], None if last else norm_gains[l + 1, 0], 0.5)
    return xr.reshape(batch, seq, d_model)
```

```python
import functools
import math

import jax
import jax.numpy as jnp
from jax import lax
from jax.experimental import pallas as pl
from jax.experimental.pallas import tpu as pltpu

F32 = jnp.float32
BF16 = jnp.bfloat16

N_MIXERS = 4
HEAD_DIM = 128
HEADS = 8
GROUP = HEADS * HEAD_DIM
DIFF_DIM = HEAD_DIM // 2
CHUNK = 128
CONV_WIDTH = 4
NORM_EPS = 1e-6
ROPE_THETA = 10000.0
RET_THETA = 10000.0
GATE_ROWS = 32
NEG_BIG = -1e30
LOG2E = 1.4426950408889634
ATTN_GROUP = 4
HEADS_PER_STEP = 2

GA_FQ, GA_FK, GA_FV = range(3)
GB_DQ, GB_DK, GB_DV, GB_RQ, GB_RK, GB_RV, GB_RG, GB_MQ, GB_MK, GB_MV, GB_MO = range(11)

VMEM_LIMIT_BYTES = 56 * 1024 * 1024

TILES = dict(
    norm_rows=256,
    ffn_up=(2048, 256),
    ffn_down=(512, 512),
    in_proj=(1024, 512),
    out_proj=(1024, 512),
    gates=1024,
    rotary=512,
    attn_q=512,
    recurrent=1024,
)


def _tile(dim, want):
    t = min(dim, want)
    while dim % t:
        t -= 128 if t > 128 else 8
    return t


def _params(*sem):
    return pltpu.CompilerParams(dimension_semantics=sem, vmem_limit_bytes=VMEM_LIMIT_BYTES)


def _rms(x):
    return x * lax.rsqrt(jnp.mean(x * x, axis=-1, keepdims=True) + NORM_EPS)


def _norm_kernel(x_ref, g_ref, o_ref):
    o_ref[...] = (_rms(x_ref[...]) * g_ref[...]).astype(o_ref.dtype)


def rms_norm_rows(x, gain):
    t, d = x.shape
    tm = _tile(t, TILES["norm_rows"])
    return pl.pallas_call(
        _norm_kernel,
        grid=(t // tm,),
        in_specs=[pl.BlockSpec((tm, d), lambda i: (i, 0)),
                  pl.BlockSpec((1, d), lambda i: (0, 0))],
        out_specs=pl.BlockSpec((tm, d), lambda i: (i, 0)),
        out_shape=jax.ShapeDtypeStruct((t, d), BF16),
        compiler_params=_params("parallel"),
        name="rms_norm_rows",
    )(x, gain.reshape(1, d))


def _resid_norm_kernel(x_ref, y_ref, gp_ref, gn_ref, xo_ref, xn_ref, *, coef):
    yn = _rms(y_ref[...].astype(F32)) * gp_ref[...]
    x = x_ref[...] + coef * yn
    xo_ref[...] = x
    xn_ref[...] = (_rms(x) * gn_ref[...]).astype(xn_ref.dtype)


def _resid_kernel(x_ref, y_ref, gp_ref, xo_ref, *, coef):
    yn = _rms(y_ref[...].astype(F32)) * gp_ref[...]
    xo_ref[...] = x_ref[...] + coef * yn


def resid_norm(x, y, g_post, g_next, coef):
    t, d = x.shape
    tm = _tile(t, TILES["norm_rows"])
    row = pl.BlockSpec((tm, d), lambda i: (i, 0))
    vec = pl.BlockSpec((1, d), lambda i: (0, 0))
    if g_next is None:
        return pl.pallas_call(
            functools.partial(_resid_kernel, coef=coef),
            grid=(t // tm,),
            in_specs=[row, row, vec],
            out_specs=row,
            out_shape=jax.ShapeDtypeStruct((t, d), F32),
            compiler_params=_params("parallel"),
            name="resid_last",
        )(x, y, g_post.reshape(1, d)), None
    return pl.pallas_call(
        functools.partial(_resid_norm_kernel, coef=coef),
        grid=(t // tm,),
        in_specs=[row, row, vec, vec],
        out_specs=[row, row],
        out_shape=[jax.ShapeDtypeStruct((t, d), F32), jax.ShapeDtypeStruct((t, d), BF16)],
        compiler_params=_params("parallel"),
        name="resid_norm",
    )(x, y, g_post.reshape(1, d), g_next.reshape(1, d))


def _ffn_up_kernel(x_ref, wg_ref, wu_ref, o_ref):
    x = x_ref[...]
    g = jnp.dot(x, wg_ref[...].astype(x.dtype), preferred_element_type=F32)
    u = jnp.dot(x, wu_ref[...].astype(x.dtype), preferred_element_type=F32)
    o_ref[...] = (g * jax.nn.sigmoid(g) * u).astype(o_ref.dtype)


def ffn_up(xn, w_gate, w_up, layer):
    t, d = xn.shape
    f = w_gate.shape[2]
    bm, bn = _tile(t, TILES["ffn_up"][0]), _tile(f, TILES["ffn_up"][1])
    w_spec = pl.BlockSpec((None, d, bn), lambda i, j: (layer, 0, j))
    x_spec = pl.BlockSpec((bm, d), lambda i, j: (i, 0), pipeline_mode=pl.Buffered(1))
    return pl.pallas_call(
        _ffn_up_kernel,
        grid=(t // bm, f // bn),
        in_specs=[x_spec, w_spec, w_spec],
        out_specs=pl.BlockSpec((bm, bn), lambda i, j: (i, j)),
        out_shape=jax.ShapeDtypeStruct((t, f), BF16),
        compiler_params=_params("parallel", "arbitrary"),
        name="ffn_up",
    )(xn, w_gate, w_up)


def _matmul_kernel(a_ref, w_ref, o_ref):
    o_ref[...] = jnp.dot(a_ref[...], w_ref[...], preferred_element_type=F32).astype(o_ref.dtype)


def matmul(a, w, tiles, name):
    m, k = a.shape
    n = w.shape[1]
    bm, bn = _tile(m, TILES[tiles][0]), _tile(n, TILES[tiles][1])
    return pl.pallas_call(
        _matmul_kernel,
        grid=(m // bm, n // bn),
        in_specs=[pl.BlockSpec((bm, k), lambda i, j: (i, 0)),
                  pl.BlockSpec((k, bn), lambda i, j: (0, j))],
        out_specs=pl.BlockSpec((bm, bn), lambda i, j: (i, j)),
        out_shape=jax.ShapeDtypeStruct((m, n), BF16),
        compiler_params=_params("parallel", "arbitrary"),
        name=name,
    )(a, w)


def _out_proj_kernel(a0_ref, a1_ref, a2_ref, a3_ref, w_ref, o_ref):
    acc = jnp.dot(a0_ref[...], w_ref[0], preferred_element_type=F32)
    acc += jnp.dot(a1_ref[...], w_ref[1], preferred_element_type=F32)
    acc += jnp.dot(a2_ref[...], w_ref[2], preferred_element_type=F32)
    acc += jnp.dot(a3_ref[...], w_ref[3], preferred_element_type=F32)
    o_ref[...] = acc.astype(o_ref.dtype)


def out_proj(parts, w_out4):
    t, gw = parts[0].shape
    n = w_out4.shape[2]
    bm, bn = _tile(t, TILES["out_proj"][0]), _tile(n, TILES["out_proj"][1])
    a_spec = pl.BlockSpec((bm, gw), lambda i, j: (i, 0))
    return pl.pallas_call(
        _out_proj_kernel,
        grid=(t // bm, n // bn),
        in_specs=[a_spec, a_spec, a_spec, a_spec,
                  pl.BlockSpec((N_MIXERS, gw, bn), lambda i, j: (0, 0, j))],
        out_specs=pl.BlockSpec((bm, bn), lambda i, j: (i, j)),
        out_shape=jax.ShapeDtypeStruct((t, n), BF16),
        compiler_params=_params("parallel", "arbitrary"),
        name="out_proj",
    )(*parts, w_out4)


def _log_sigmoid(z):
    return jnp.minimum(z, 0.0) - jnp.log(1.0 + jnp.exp(-jnp.abs(z)))


def _chunk_cumsum_lanes(x):
    lane = lax.broadcasted_iota(jnp.int32, x.shape, 1) & (CHUNK - 1)
    s = 1
    while s < CHUNK:
        x = x + jnp.where(lane >= s, pltpu.roll(x, s, 1), 0.0)
        s *= 2
    return x


def _gates_kernel(hn_ref, wt_ref, b_ref, o_ref, carry_sc):
    @pl.when(pl.program_id(1) == 0)
    def _():
        carry_sc[...] = jnp.zeros_like(carry_sc)

    z = lax.dot_general(wt_ref[...], hn_ref[...], (((1,), (1,)), ((), ())),
                        preferred_element_type=F32) + b_ref[...]
    tm = z.shape[1]
    fox = _chunk_cumsum_lanes(_log_sigmoid(z[0:HEADS]))
    carry = carry_sc[...]
    for c in range(tm // CHUNK):
        seg = fox[:, c * CHUNK:(c + 1) * CHUNK] + carry
        o_ref[0:HEADS, c * CHUNK:(c + 1) * CHUNK] = seg
        carry = jnp.broadcast_to(seg[:, CHUNK - 1:CHUNK], carry.shape)
    carry_sc[...] = carry
    o_ref[HEADS:2 * HEADS, :] = z[HEADS:2 * HEADS]
    o_ref[2 * HEADS:3 * HEADS, :] = _chunk_cumsum_lanes(_log_sigmoid(z[2 * HEADS:3 * HEADS]))
    o_ref[3 * HEADS:, :] = jnp.zeros((GATE_ROWS - 3 * HEADS, tm), F32)


def gates(hn, w_gates_t, gate_bias, batch):
    t, d = hn.shape
    s = t // batch
    tm = _tile(s, TILES["gates"])
    ns = s // tm
    return pl.pallas_call(
        _gates_kernel,
        grid=(batch, ns),
        in_specs=[pl.BlockSpec((tm, d), lambda b, i: (b * ns + i, 0)),
                  pl.BlockSpec((GATE_ROWS, d), lambda b, i: (0, 0)),
                  pl.BlockSpec((GATE_ROWS, 1), lambda b, i: (0, 0))],
        out_specs=pl.BlockSpec((GATE_ROWS, tm), lambda b, i: (0, b * ns + i)),
        out_shape=jax.ShapeDtypeStruct((GATE_ROWS, t), F32),
        scratch_shapes=[pltpu.VMEM((HEADS, CHUNK), F32)],
        compiler_params=_params("parallel", "arbitrary"),
        name="gates",
    )(hn, w_gates_t, gate_bias)


DIFF_PAIR = DIFF_DIM // 2
RET_PAIR = 1


def _rotate(x, cos, sin_signed, pair):
    lane = lax.broadcasted_iota(jnp.int32, x.shape, 1)
    partner = jnp.where((lane & pair) != 0, pltpu.roll(x, pair, 1), pltpu.roll(x, HEAD_DIM - pair, 1))
    return x * cos + partner * sin_signed


def _rotary_kernel(q_ref, k_ref, cos_ref, sin_ref, qo_ref, ko_ref, *, q_scale):
    cos, sin = cos_ref[...], sin_ref[...]
    for h in range(HEADS):
        sl = slice(h * HEAD_DIM, (h + 1) * HEAD_DIM)
        qo_ref[:, sl] = (_rotate(q_ref[:, sl].astype(F32), cos, sin, DIFF_PAIR) * q_scale).astype(qo_ref.dtype)
        ko_ref[:, sl] = _rotate(k_ref[:, sl].astype(F32), cos, sin, DIFF_PAIR).astype(ko_ref.dtype)


def rotary_qk(proj, gq, gk, cos, sin, batch, q_scale):
    t = proj.shape[0]
    s = t // batch
    tm = _tile(s, TILES["rotary"])
    ns = s // tm
    tab = pl.BlockSpec((tm, HEAD_DIM), lambda i: (i % ns, 0))
    out = pl.BlockSpec((tm, GROUP), lambda i: (i, 0))
    return pl.pallas_call(
        functools.partial(_rotary_kernel, q_scale=q_scale),
        grid=(t // tm,),
        in_specs=[pl.BlockSpec((tm, GROUP), lambda i: (i, gq)),
                  pl.BlockSpec((tm, GROUP), lambda i: (i, gk)), tab, tab],
        out_specs=[out, out],
        out_shape=[jax.ShapeDtypeStruct((t, GROUP), BF16)] * 2,
        compiler_params=_params("parallel"),
        name="rotary_qk",
    )(proj, proj, cos, sin)


def _online_softmax_step(s, v, m_sc, acc_sc):
    m_old = m_sc[...]
    m_new = jnp.maximum(m_old, jnp.max(s, axis=-1, keepdims=True))
    p = jnp.exp2(s - jnp.tile(m_new, (1, s.shape[1] // HEAD_DIM)))
    alpha = jnp.exp2(m_old - m_new)
    v_ones = jnp.concatenate([v, jnp.ones_like(v)], axis=1)
    acc_sc[...] = (jnp.tile(alpha, (1, 2)) * acc_sc[...]
                   + jnp.dot(p.astype(v.dtype), v_ones, preferred_element_type=F32))
    m_sc[...] = m_new


def _softmax_init(m_sc, acc_sc):
    m_sc[...] = jnp.full_like(m_sc, NEG_BIG)
    acc_sc[...] = jnp.zeros_like(acc_sc)


def _softmax_result(acc_sc):
    acc = acc_sc[...]
    return acc[:, :HEAD_DIM] / acc[:, HEAD_DIM:]


def _qk(q, k):
    return lax.dot_general(q, k, (((1,), (1,)), ((), ())), preferred_element_type=F32)


def _causal_sweep(qi, logits_fn, values_fn, tq, m_sc, acc_sc):
    def full(j):
        _online_softmax_step(logits_fn(j), values_fn(j), m_sc, acc_sc)

    def diagonal():
        s = logits_fn(qi)
        row = lax.broadcasted_iota(jnp.int32, s.shape, 0) & (tq - 1)
        col = lax.broadcasted_iota(jnp.int32, s.shape, 1)
        _online_softmax_step(jnp.where(col <= row, s, NEG_BIG), values_fn(qi), m_sc, acc_sc)

    def group(j, carry):
        for u in range(ATTN_GROUP):
            full(ATTN_GROUP * j + u)
        return carry

    _softmax_init(m_sc, acc_sc)
    lax.fori_loop(0, qi // ATTN_GROUP, group, 0)
    rest = qi % ATTN_GROUP
    for r in range(ATTN_GROUP):
        @pl.when(rest == r)
        def _(r=r):
            for u in range(r):
                full(qi - r + u)
            diagonal()


def _fox_kernel(q_ref, k_ref, v_ref, c_ref, o_ref, m_sc, acc_sc, *, tq, scale):
    qi = pl.program_id(2)
    q = (q_ref[...].astype(F32) * scale).astype(q_ref.dtype)
    c0 = c_ref[:, pl.ds(pl.multiple_of(qi * tq, tq), CHUNK)][:, 0:1]

    def logits(j):
        start = pl.multiple_of(j * tq, tq)
        return _qk(q, k_ref[pl.ds(start, tq), :]) + (c0 - c_ref[:, pl.ds(start, tq)]) * LOG2E

    def values(j):
        return v_ref[pl.ds(pl.multiple_of(j * tq, tq), tq), :]

    _causal_sweep(qi, logits, values, tq, m_sc, acc_sc)
    o_ref[...] = _softmax_result(acc_sc).astype(o_ref.dtype)


def fox_attention(proj, cum3, batch):
    t = proj.shape[0]
    s = t // batch
    tq = _tile(s, TILES["attn_q"])
    assert tq & (tq - 1) == 0, "the causal mask takes row mod tq with a bit mask"
    nq = s // tq
    return pl.pallas_call(
        functools.partial(_fox_kernel, tq=tq, scale=HEAD_DIM ** -0.5 * LOG2E),
        grid=(batch, HEADS, nq),
        in_specs=[pl.BlockSpec((tq, HEAD_DIM), lambda b, h, i: (b * nq + i, GA_FQ * HEADS + h)),
                  pl.BlockSpec((s, HEAD_DIM), lambda b, h, i: (b, GA_FK * HEADS + h)),
                  pl.BlockSpec((s, HEAD_DIM), lambda b, h, i: (b, GA_FV * HEADS + h)),
                  pl.BlockSpec((None, 1, s), lambda b, h, i: (h, 0, b))],
        out_specs=pl.BlockSpec((tq, HEAD_DIM), lambda b, h, i: (b * nq + i, h)),
        out_shape=jax.ShapeDtypeStruct((t, GROUP), BF16),
        scratch_shapes=[pltpu.VMEM((tq, HEAD_DIM), F32), pltpu.VMEM((tq, 2 * HEAD_DIM), F32)],
        compiler_params=_params("parallel", "parallel", "arbitrary"),
        name="fox_attention",
    )(proj, proj, proj, cum3)


def _diff_kernel(q_ref, k_ref, v_ref, lam_ref, g_ref, o_ref, m_sc, acc_sc, *, tq, lam_init):
    qi = pl.program_id(2)
    q = q_ref[...]
    lane = lax.broadcasted_iota(jnp.int32, q.shape, 1)
    first_map = lane < DIFF_DIM
    zero = jnp.zeros_like(q)
    q12 = jnp.concatenate([jnp.where(first_map, q, zero), jnp.where(first_map, zero, q)], axis=0)

    def logits(j):
        return _qk(q12, k_ref[pl.ds(pl.multiple_of(j * tq, tq), tq), :])

    def values(j):
        return v_ref[pl.ds(pl.multiple_of(j * tq, tq), tq), :]

    _causal_sweep(qi, logits, values, tq, m_sc, acc_sc)
    o = _softmax_result(acc_sc)
    lp = lam_ref[...]
    lam = (jnp.exp(jnp.sum(lp[0:1] * lp[1:2], axis=-1, keepdims=True))
           - jnp.exp(jnp.sum(lp[2:3] * lp[3:4], axis=-1, keepdims=True)) + lam_init)
    d = o[:tq] - lam * o[tq:]
    o_ref[...] = (_rms(d) * g_ref[...] * (1.0 - lam_init)).astype(o_ref.dtype)


def diff_attention(dq, dk, proj, lam_params, subln_g, batch, lam_init):
    t = proj.shape[0]
    s = t // batch
    tq = _tile(s, TILES["attn_q"])
    assert tq & (tq - 1) == 0, "the causal mask takes row mod tq with a bit mask"
    nq = s // tq
    return pl.pallas_call(
        functools.partial(_diff_kernel, tq=tq, lam_init=lam_init),
        grid=(batch, HEADS, nq),
        in_specs=[pl.BlockSpec((tq, HEAD_DIM), lambda b, h, i: (b * nq + i, h)),
                  pl.BlockSpec((s, HEAD_DIM), lambda b, h, i: (b, h)),
                  pl.BlockSpec((s, HEAD_DIM), lambda b, h, i: (b, GB_DV * HEADS + h)),
                  pl.BlockSpec((4, DIFF_DIM), lambda b, h, i: (0, 0)),
                  pl.BlockSpec((1, HEAD_DIM), lambda b, h, i: (0, 0))],
        out_specs=pl.BlockSpec((tq, HEAD_DIM), lambda b, h, i: (b * nq + i, h)),
        out_shape=jax.ShapeDtypeStruct((t, GROUP), BF16),
        scratch_shapes=[pltpu.VMEM((2 * tq, HEAD_DIM), F32), pltpu.VMEM((2 * tq, 2 * HEAD_DIM), F32)],
        compiler_params=_params("parallel", "parallel", "arbitrary"),
        name="diff_attention",
    )(dq, dk, proj, lam_params, subln_g.reshape(1, HEAD_DIM))


def _tn_dot(a, b):
    return lax.dot_general(a, b, (((0,), (0,)), ((), ())), preferred_element_type=F32)


def _retention_kernel(q_ref, k_ref, v_ref, g_ref, cos_ref, sin_ref, o_ref, state_sc, *, n_chunks):
    @pl.when(pl.program_id(2) == 0)
    def _():
        state_sc[...] = jnp.zeros_like(state_sc)

    rowi = lax.broadcasted_iota(jnp.int32, (CHUNK, CHUNK), 0)
    coli = lax.broadcasted_iota(jnp.int32, (CHUNK, CHUNK), 1)
    rel = (rowi - coli).astype(F32)
    pos = lax.broadcasted_iota(jnp.int32, (CHUNK, 1), 0).astype(F32)
    k_scale = HEAD_DIM ** -0.5

    decays = []
    for hh in range(HEADS_PER_STEP):
        h = (pl.program_id(1) * HEADS_PER_STEP + hh).astype(F32)
        log_gamma = jnp.log(1.0 - jnp.exp2(jnp.full((1, 1), -5.0, F32) - h))
        decays.append((jnp.where(rel >= 0, jnp.exp(jnp.maximum(rel, 0.0) * log_gamma), 0.0),
                       jnp.exp((pos + 1.0) * log_gamma),
                       jnp.exp((CHUNK - 1.0 - pos) * log_gamma),
                       jnp.exp(CHUNK * log_gamma)))

    states = [state_sc[hh] for hh in range(HEADS_PER_STEP)]
    for c in range(n_chunks):
        rows = slice(c * CHUNK, (c + 1) * CHUNK)
        cos, sin = cos_ref[rows, :], sin_ref[rows, :]
        for hh in range(HEADS_PER_STEP):
            intra, q_decay, k_decay, chunk_decay = decays[hh]
            cols = slice(hh * HEAD_DIM, (hh + 1) * HEAD_DIM)
            q = _rotate(q_ref[rows, cols].astype(F32), cos, sin, RET_PAIR).astype(BF16)
            kf = _rotate(k_ref[rows, cols].astype(F32), cos, sin, RET_PAIR) * k_scale
            v = v_ref[rows, cols]
            scores = _qk(q, kf.astype(BF16)) * intra
            o = (jnp.dot(scores.astype(BF16), v, preferred_element_type=F32)
                 + jnp.dot(q, states[hh].astype(BF16), preferred_element_type=F32) * q_decay)
            states[hh] = chunk_decay * states[hh] + _tn_dot((kf * k_decay).astype(BF16), v)
            g = g_ref[rows, cols].astype(F32)
            o_ref[rows, cols] = (g * jax.nn.sigmoid(g) * _rms(o)).astype(o_ref.dtype)
    for hh in range(HEADS_PER_STEP):
        state_sc[hh] = states[hh]


def _head_cols(tb, nb, group):
    width = HEADS_PER_STEP * HEAD_DIM
    return pl.BlockSpec((tb, width), lambda b, hp, i: (b * nb + i, group * (GROUP // width) + hp))


def retention(proj, cos, sin, batch):
    t = proj.shape[0]
    s = t // batch
    tb = _tile(s, TILES["recurrent"])
    nb = s // tb
    tab = pl.BlockSpec((tb, HEAD_DIM), lambda b, hp, i: (i, 0))
    return pl.pallas_call(
        functools.partial(_retention_kernel, n_chunks=tb // CHUNK),
        grid=(batch, HEADS // HEADS_PER_STEP, nb),
        in_specs=[_head_cols(tb, nb, GB_RQ), _head_cols(tb, nb, GB_RK), _head_cols(tb, nb, GB_RV),
                  _head_cols(tb, nb, GB_RG), tab, tab],
        out_specs=_head_cols(tb, nb, 0),
        out_shape=jax.ShapeDtypeStruct((t, GROUP), BF16),
        scratch_shapes=[pltpu.VMEM((HEADS_PER_STEP, HEAD_DIM, HEAD_DIM), F32)],
        compiler_params=_params("parallel", "parallel", "arbitrary"),
        name="retention",
    )(proj, proj, proj, proj, cos, sin)


def _causal_conv_silu(x, prev, w):
    tb = x.shape[0]
    ext = jnp.concatenate([prev, x], axis=0)
    acc = None
    for i in range(CONV_WIDTH):
        off = 8 - (CONV_WIDTH - 1) + i
        term = ext[off:off + tb, :] * w[i:i + 1, :]
        acc = term if acc is None else acc + term
    return acc * jax.nn.sigmoid(acc)


def _mlstm_kernel(q_ref, k_ref, v_ref, og_ref, wq_ref, wk_ref, li_ref, bc_ref, gcol_ref, ng_ref,
                  o_ref, c_sc, n_sc, m_sc, pq_sc, pk_sc, *, n_chunks):
    @pl.when(pl.program_id(2) == 0)
    def _():
        c_sc[...] = jnp.zeros_like(c_sc)
        n_sc[...] = jnp.zeros_like(n_sc)
        m_sc[...] = jnp.zeros_like(m_sc)
        pq_sc[...] = jnp.zeros_like(pq_sc)
        pk_sc[...] = jnp.zeros_like(pk_sc)

    tb = q_ref.shape[0]
    xq = q_ref[...].astype(F32)
    xk = k_ref[...].astype(F32)
    qf_all = _causal_conv_silu(xq, pq_sc[...], wq_ref[...])
    kf_all = _causal_conv_silu(xk, pk_sc[...], wk_ref[...]) * (HEAD_DIM ** -0.5)
    pq_sc[...] = xq[tb - 8:, :]
    pk_sc[...] = xk[tb - 8:, :]

    lane = lax.broadcasted_iota(jnp.int32, (CHUNK, GATE_ROWS), 1)
    rowi = lax.broadcasted_iota(jnp.int32, (CHUNK, CHUNK), 0)
    coli = lax.broadcasted_iota(jnp.int32, (CHUNK, CHUNK), 1)
    causal = coli <= rowi

    carries = [(c_sc[hh], n_sc[hh], m_sc[hh]) for hh in range(HEADS_PER_STEP)]
    for c in range(n_chunks):
        rows = slice(c * CHUNK, (c + 1) * CHUNK)
        gc = gcol_ref[rows, :]
        for hh in range(HEADS_PER_STEP):
            h = pl.program_id(1) * HEADS_PER_STEP + hh
            c_mat, n_vec, m = carries[hh]
            cols = slice(hh * HEAD_DIM, (hh + 1) * HEAD_DIM)
            qf, kf = qf_all[rows, cols], kf_all[rows, cols]
            q, k, v = qf.astype(BF16), kf.astype(BF16), v_ref[rows, cols]
            li_row, bc_row = li_ref[hh, :, rows], bc_ref[hh, :, rows]
            li_col = jnp.sum(jnp.where(lane == HEADS + h, gc, 0.0), axis=1, keepdims=True)
            bc_col = jnp.sum(jnp.where(lane == 2 * HEADS + h, gc, 0.0), axis=1, keepdims=True)
            b_last = bc_row[:, CHUNK - 1:CHUNK]

            dlog = jnp.where(causal, bc_col - bc_row + li_row, -jnp.inf)
            inter = bc_col + m
            m_t = jnp.maximum(inter, jnp.max(dlog, axis=-1, keepdims=True))
            w_intra = jnp.exp(dlog - m_t)
            w_inter = jnp.exp(inter - m_t)
            s_qk = _qk(q, k) * w_intra
            num = (jnp.dot(s_qk.astype(BF16), v, preferred_element_type=F32)
                   + w_inter * jnp.dot(q, c_mat.astype(BF16), preferred_element_type=F32))
            den = (jnp.sum(s_qk, axis=-1, keepdims=True)
                   + w_inter * jnp.sum(qf * n_vec, axis=-1, keepdims=True))
            h_t = num / jnp.maximum(jnp.abs(den), jnp.exp(-m_t))

            end_log = b_last - bc_col + li_col
            m_new = jnp.maximum(b_last + m, jnp.max(end_log, axis=0, keepdims=True))
            w_end = jnp.exp(end_log - m_new)
            carry_decay = jnp.exp(b_last + m - m_new)
            kw = kf * w_end
            carries[hh] = (carry_decay * c_mat + _tn_dot(kw.astype(BF16), v),
                           carry_decay * n_vec + jnp.sum(kw, axis=0, keepdims=True),
                           m_new)

            og = og_ref[rows, cols].astype(F32)
            o_ref[rows, cols] = (jax.nn.sigmoid(og) * (_rms(h_t) * ng_ref[:, cols])).astype(o_ref.dtype)
    for hh in range(HEADS_PER_STEP):
        c_sc[hh], n_sc[hh], m_sc[hh] = carries[hh]


def mlstm(proj, conv_w, gates3, gates_col, norm_g, batch):
    t = proj.shape[0]
    s = t // batch
    tb = _tile(s, TILES["recurrent"])
    nb = s // tb
    hps = HEADS_PER_STEP
    width = hps * HEAD_DIM
    n_hp = HEADS // hps
    return pl.pallas_call(
        functools.partial(_mlstm_kernel, n_chunks=tb // CHUNK),
        grid=(batch, n_hp, nb),
        in_specs=[_head_cols(tb, nb, GB_MQ), _head_cols(tb, nb, GB_MK), _head_cols(tb, nb, GB_MV),
                  _head_cols(tb, nb, GB_MO),
                  pl.BlockSpec((CONV_WIDTH, width), lambda b, hp, i: (0, hp)),
                  pl.BlockSpec((CONV_WIDTH, width), lambda b, hp, i: (0, n_hp + hp)),
                  pl.BlockSpec((hps, 1, tb), lambda b, hp, i: (n_hp + hp, 0, b * nb + i)),
                  pl.BlockSpec((hps, 1, tb), lambda b, hp, i: (2 * n_hp + hp, 0, b * nb + i)),
                  pl.BlockSpec((tb, GATE_ROWS), lambda b, hp, i: (b * nb + i, 0)),
                  pl.BlockSpec((1, width), lambda b, hp, i: (0, hp))],
        out_specs=_head_cols(tb, nb, 0),
        out_shape=jax.ShapeDtypeStruct((t, GROUP), BF16),
        scratch_shapes=[pltpu.VMEM((hps, HEAD_DIM, HEAD_DIM), F32), pltpu.VMEM((hps, 1, HEAD_DIM), F32),
                        pltpu.VMEM((hps, 1, 1), F32), pltpu.VMEM((8, width), F32),
                        pltpu.VMEM((8, width), F32)],
        compiler_params=_params("parallel", "parallel", "arbitrary"),
        name="mlstm",
    )(proj, proj, proj, proj, conv_w, conv_w, gates3, gates3, gates_col, norm_g.reshape(1, GROUP))


W_IN_A = N_MIXERS - 1
W_IN_B = 11
W_IN_B_START = W_IN_A * GROUP + HEADS
W_IN_GATES = W_IN_B_START + W_IN_B * GROUP


def _prep_w_in(w_in_l):
    wide_a = w_in_l[:, :W_IN_A * GROUP].astype(BF16)
    wide_b = w_in_l[:, W_IN_B_START:W_IN_GATES].astype(BF16)
    d = w_in_l.shape[0]
    gate_t = jnp.concatenate([w_in_l[:, W_IN_A * GROUP:W_IN_B_START], w_in_l[:, W_IN_GATES:],
                              jnp.zeros((d, GATE_ROWS - 3 * HEADS), w_in_l.dtype)], axis=1).T.astype(BF16)
    return wide_a, wide_b, gate_t


def _rotary_tables(seq):
    pos = jnp.arange(seq, dtype=F32)[:, None]
    inv_d = 1.0 / (ROPE_THETA ** (jnp.arange(0, DIFF_DIM, 2, dtype=F32) / DIFF_DIM))
    ang_d = pos * inv_d[None, :]
    cos_d = jnp.tile(jnp.cos(ang_d), (1, 4))
    sin_d = jnp.tile(jnp.concatenate([-jnp.sin(ang_d), jnp.sin(ang_d)], axis=1), (1, 2))
    inv_r = 1.0 / (RET_THETA ** jnp.linspace(0.0, 1.0, HEAD_DIM // 2, dtype=F32))
    ang_r = pos * inv_r[None, :]
    cos_r = jnp.repeat(jnp.cos(ang_r), 2, axis=1)
    sin_r = jnp.stack([-jnp.sin(ang_r), jnp.sin(ang_r)], axis=-1).reshape(seq, HEAD_DIM)
    return cos_d, sin_d, cos_r, sin_r


def kernel(x, norm_gains, ffn_a_gate, ffn_a_up, ffn_a_down, w_in, w_out, fox_fgate_b,
           diff_lq1, diff_lk1, diff_lq2, diff_lk2, diff_subln_g, mlstm_conv_w,
           mlstm_igate_b, mlstm_fgate_b, mlstm_norm_g, ffn_b_gate, ffn_b_up, ffn_b_down):
    batch, seq, d_model = x.shape
    depth = norm_gains.shape[0]
    t = batch * seq
    assert d_model == N_MIXERS * GROUP and seq % CHUNK == 0
    cos_d, sin_d, cos_r, sin_r = _rotary_tables(seq)

    xr = x.reshape(t, d_model)
    xn = rms_norm_rows(xr, norm_gains[0, 0])
    for l in range(depth):
        lam_init = 0.8 - 0.6 * math.exp(-0.3 * l)
        g = norm_gains[l]
        last = l == depth - 1

        hmid = ffn_up(xn, ffn_a_gate, ffn_a_up, l)
        y = matmul(hmid, ffn_a_down[l].astype(BF16), "ffn_down", "ffn_down")
        xr, xn = resid_norm(xr, y, g[1], g[2], 0.5)

        w_wide_a, w_wide_b, w_gate_t = _prep_w_in(w_in[l])
        proj_a = matmul(xn, w_wide_a, "in_proj", "in_proj_fox")
        proj_b = matmul(xn, w_wide_b, "in_proj", "in_proj")
        gate_bias = jnp.concatenate([fox_fgate_b[l], mlstm_igate_b[l], mlstm_fgate_b[l],
                                     jnp.zeros((GATE_ROWS - 3 * HEADS,), F32)]).reshape(GATE_ROWS, 1)
        gts = gates(xn, w_gate_t, gate_bias, batch)
        gates3 = gts.reshape(GATE_ROWS, 1, t)
        gates_col = gts.T

        fox_out = fox_attention(proj_a, gates3, batch)
        dq, dk = rotary_qk(proj_b, GB_DQ, GB_DK, cos_d, sin_d, batch, DIFF_DIM ** -0.5 * LOG2E)
        lam_params = jnp.stack([diff_lq1[l], diff_lk1[l], diff_lq2[l], diff_lk2[l]]).astype(F32)
        diff_out = diff_attention(dq, dk, proj_b, lam_params, diff_subln_g[l], batch, lam_init)
        ret_out = retention(proj_b, cos_r, sin_r, batch)
        mlstm_out = mlstm(proj_b, mlstm_conv_w[l], gates3, gates_col, mlstm_norm_g[l], batch)

        w_out4 = w_out[l].astype(BF16).reshape(N_MIXERS, GROUP, d_model)
        mix = out_proj([fox_out, diff_out, ret_out, mlstm_out], w_out4)
        xr, xn = resid_norm(xr, mix, g[3], g[4], 1.0)

        hmid = ffn_up(xn, ffn_b_gate, ffn_b_up, l)
        y = matmul(hmid, ffn_b_down[l].astype(BF16), "ffn_down", "ffn_down")
        xr, xn = resid_norm(xr, y, g[5], None if last else norm_gains[l + 1, 0], 0.5)
    return xr.reshape(batch, seq, d_model)
```

```python
import functools
import math

import jax
import jax.numpy as jnp
from jax import lax
from jax.experimental import pallas as pl
from jax.experimental.pallas import tpu as pltpu

F32 = jnp.float32
BF16 = jnp.bfloat16

N_MIXERS = 4
HEAD_DIM = 128
HEADS = 8
GROUP = HEADS * HEAD_DIM
DIFF_DIM = HEAD_DIM // 2
CHUNK = 128
CONV_WIDTH = 4
NORM_EPS = 1e-6
ROPE_THETA = 10000.0
RET_THETA = 10000.0
GATE_ROWS = 32
NEG_BIG = -1e30
LOG2E = 1.4426950408889634
ATTN_GROUP = 4
HEADS_PER_STEP = 2

GA_FQ, GA_FK, GA_FV = range(3)
GB_DQ, GB_DK, GB_DV, GB_RQ, GB_RK, GB_RV, GB_RG, GB_MQ, GB_MK, GB_MV, GB_MO = range(11)

VMEM_LIMIT_BYTES = 56 * 1024 * 1024

TILES = dict(
    norm_rows=256,
    ffn_up=(2048, 256),
    ffn_down=(512, 512),
    in_proj=(1024, 512),
    out_proj=(1024, 512),
    gates=1024,
    rotary=512,
    attn_q=512,
    recurrent=1024,
)


def _tile(dim, want):
    t = min(dim, want)
    while dim % t:
        t -= 128 if t > 128 else 8
    return t


def _params(*sem):
    return pltpu.CompilerParams(dimension_semantics=sem, vmem_limit_bytes=VMEM_LIMIT_BYTES)


def _rms(x):
    return x * lax.rsqrt(jnp.mean(x * x, axis=-1, keepdims=True) + NORM_EPS)


def _norm_kernel(x_ref, g_ref, o_ref):
    o_ref[...] = (_rms(x_ref[...]) * g_ref[...]).astype(o_ref.dtype)


def rms_norm_rows(x, gain):
    t, d = x.shape
    tm = _tile(t, TILES["norm_rows"])
    return pl.pallas_call(
        _norm_kernel,
        grid=(t // tm,),
        in_specs=[pl.BlockSpec((tm, d), lambda i: (i, 0)),
                  pl.BlockSpec((1, d), lambda i: (0, 0))],
        out_specs=pl.BlockSpec((tm, d), lambda i: (i, 0)),
        out_shape=jax.ShapeDtypeStruct((t, d), BF16),
        compiler_params=_params("parallel"),
        name="rms_norm_rows",
    )(x, gain.reshape(1, d))


def _resid_norm_kernel(x_ref, y_ref, gp_ref, gn_ref, xo_ref, xn_ref, *, coef):
    yn = _rms(y_ref[...].astype(F32)) * gp_ref[...]
    x = x_ref[...] + coef * yn
    xo_ref[...] = x
    xn_ref[...] = (_rms(x) * gn_ref[...]).astype(xn_ref.dtype)


def _resid_kernel(x_ref, y_ref, gp_ref, xo_ref, *, coef):
    yn = _rms(y_ref[...].astype(F32)) * gp_ref[...]
    xo_ref[...] = x_ref[...] + coef * yn


def resid_norm(x, y, g_post, g_next, coef):
    t, d = x.shape
    tm = _tile(t, TILES["norm_rows"])
    row = pl.BlockSpec((tm, d), lambda i: (i, 0))
    vec = pl.BlockSpec((1, d), lambda i: (0, 0))
    if g_next is None:
        return pl.pallas_call(
            functools.partial(_resid_kernel, coef=coef),
            grid=(t // tm,),
            in_specs=[row, row, vec],
            out_specs=row,
            out_shape=jax.ShapeDtypeStruct((t, d), F32),
            compiler_params=_params("parallel"),
            name="resid_last",
        )(x, y, g_post.reshape(1, d)), None
    return pl.pallas_call(
        functools.partial(_resid_norm_kernel, coef=coef),
        grid=(t // tm,),
        in_specs=[row, row, vec, vec],
        out_specs=[row, row],
        out_shape=[jax.ShapeDtypeStruct((t, d), F32), jax.ShapeDtypeStruct((t, d), BF16)],
        compiler_params=_params("parallel"),
        name="resid_norm",
    )(x, y, g_post.reshape(1, d), g_next.reshape(1, d))


def _ffn_up_kernel(x_ref, wg_ref, wu_ref, o_ref):
    x = x_ref[...]
    g = jnp.dot(x, wg_ref[...].astype(x.dtype), preferred_element_type=F32)
    u = jnp.dot(x, wu_ref[...].astype(x.dtype), preferred_element_type=F32)
    o_ref[...] = (g * jax.nn.sigmoid(g) * u).astype(o_ref.dtype)


def ffn_up(xn, w_gate, w_up, layer):
    t, d = xn.shape
    f = w_gate.shape[2]
    bm, bn = _tile(t, TILES["ffn_up"][0]), _tile(f, TILES["ffn_up"][1])
    w_spec = pl.BlockSpec((None, d, bn), lambda i, j: (layer, 0, j))
    x_spec = pl.BlockSpec((bm, d), lambda i, j: (i, 0), pipeline_mode=pl.Buffered(1))
    return pl.pallas_call(
        _ffn_up_kernel,
        grid=(t // bm, f // bn),
        in_specs=[x_spec, w_spec, w_spec],
        out_specs=pl.BlockSpec((bm, bn), lambda i, j: (i, j)),
        out_shape=jax.ShapeDtypeStruct((t, f), BF16),
        compiler_params=_params("parallel", "arbitrary"),
        name="ffn_up",
    )(xn, w_gate, w_up)


def _matmul_kernel(a_ref, w_ref, o_ref):
    o_ref[...] = jnp.dot(a_ref[...], w_ref[...], preferred_element_type=F32).astype(o_ref.dtype)


def matmul(a, w, tiles, name, layer=None):
    m, k = a.shape
    n = w.shape[-1]
    bm, bn = _tile(m, TILES[tiles][0]), _tile(n, TILES[tiles][1])
    if layer is None:
        w_spec = pl.BlockSpec((k, bn), lambda i, j: (0, j))
    else:
        w_spec = pl.BlockSpec((None, k, bn), lambda i, j: (layer, 0, j))
    return pl.pallas_call(
        _matmul_kernel,
        grid=(m // bm, n // bn),
        in_specs=[pl.BlockSpec((bm, k), lambda i, j: (i, 0)), w_spec],
        out_specs=pl.BlockSpec((bm, bn), lambda i, j: (i, j)),
        out_shape=jax.ShapeDtypeStruct((m, n), BF16),
        compiler_params=_params("parallel", "arbitrary"),
        name=name,
    )(a, w)


def _out_proj_kernel(a0_ref, a1_ref, a2_ref, a3_ref, w_ref, o_ref):
    acc = jnp.dot(a0_ref[...], w_ref[0], preferred_element_type=F32)
    acc += jnp.dot(a1_ref[...], w_ref[1], preferred_element_type=F32)
    acc += jnp.dot(a2_ref[...], w_ref[2], preferred_element_type=F32)
    acc += jnp.dot(a3_ref[...], w_ref[3], preferred_element_type=F32)
    o_ref[...] = acc.astype(o_ref.dtype)


def out_proj(parts, w_out4, layer):
    t, gw = parts[0].shape
    n = w_out4.shape[3]
    bm, bn = _tile(t, TILES["out_proj"][0]), _tile(n, TILES["out_proj"][1])
    a_spec = pl.BlockSpec((bm, gw), lambda i, j: (i, 0))
    return pl.pallas_call(
        _out_proj_kernel,
        grid=(t // bm, n // bn),
        in_specs=[a_spec, a_spec, a_spec, a_spec,
                  pl.BlockSpec((None, N_MIXERS, gw, bn), lambda i, j: (layer, 0, 0, j))],
        out_specs=pl.BlockSpec((bm, bn), lambda i, j: (i, j)),
        out_shape=jax.ShapeDtypeStruct((t, n), BF16),
        compiler_params=_params("parallel", "arbitrary"),
        name="out_proj",
    )(*parts, w_out4)


def _log_sigmoid(z):
    return jnp.minimum(z, 0.0) - jnp.log(1.0 + jnp.exp(-jnp.abs(z)))


def _chunk_cumsum_lanes(x):
    lane = lax.broadcasted_iota(jnp.int32, x.shape, 1) & (CHUNK - 1)
    s = 1
    while s < CHUNK:
        x = x + jnp.where(lane >= s, pltpu.roll(x, s, 1), 0.0)
        s *= 2
    return x


def _gates_kernel(hn_ref, wt_ref, b_ref, o_ref, carry_sc):
    @pl.when(pl.program_id(1) == 0)
    def _():
        carry_sc[...] = jnp.zeros_like(carry_sc)

    z = lax.dot_general(wt_ref[...], hn_ref[...], (((1,), (1,)), ((), ())),
                        preferred_element_type=F32) + b_ref[...]
    tm = z.shape[1]
    fox = _chunk_cumsum_lanes(_log_sigmoid(z[0:HEADS]))
    carry = carry_sc[...]
    for c in range(tm // CHUNK):
        seg = fox[:, c * CHUNK:(c + 1) * CHUNK] + carry
        o_ref[0:HEADS, c * CHUNK:(c + 1) * CHUNK] = seg
        carry = jnp.broadcast_to(seg[:, CHUNK - 1:CHUNK], carry.shape)
    carry_sc[...] = carry
    o_ref[HEADS:2 * HEADS, :] = z[HEADS:2 * HEADS]
    o_ref[2 * HEADS:3 * HEADS, :] = _chunk_cumsum_lanes(_log_sigmoid(z[2 * HEADS:3 * HEADS]))
    o_ref[3 * HEADS:, :] = jnp.zeros((GATE_ROWS - 3 * HEADS, tm), F32)


def gates(hn, w_gates_t, gate_bias, batch):
    t, d = hn.shape
    s = t // batch
    tm = _tile(s, TILES["gates"])
    ns = s // tm
    return pl.pallas_call(
        _gates_kernel,
        grid=(batch, ns),
        in_specs=[pl.BlockSpec((tm, d), lambda b, i: (b * ns + i, 0)),
                  pl.BlockSpec((GATE_ROWS, d), lambda b, i: (0, 0)),
                  pl.BlockSpec((GATE_ROWS, 1), lambda b, i: (0, 0))],
        out_specs=pl.BlockSpec((GATE_ROWS, tm), lambda b, i: (0, b * ns + i)),
        out_shape=jax.ShapeDtypeStruct((GATE_ROWS, t), F32),
        scratch_shapes=[pltpu.VMEM((HEADS, CHUNK), F32)],
        compiler_params=_params("parallel", "arbitrary"),
        name="gates",
    )(hn, w_gates_t, gate_bias)


DIFF_PAIR = DIFF_DIM // 2
RET_PAIR = 1


def _rotate(x, cos, sin_signed, pair):
    lane = lax.broadcasted_iota(jnp.int32, x.shape, 1)
    partner = jnp.where((lane & pair) != 0, pltpu.roll(x, pair, 1), pltpu.roll(x, HEAD_DIM - pair, 1))
    return x * cos + partner * sin_signed


def _rotary_kernel(q_ref, k_ref, cos_ref, sin_ref, qo_ref, ko_ref, *, q_scale):
    cos, sin = cos_ref[...], sin_ref[...]
    for h in range(HEADS):
        sl = slice(h * HEAD_DIM, (h + 1) * HEAD_DIM)
        qo_ref[:, sl] = (_rotate(q_ref[:, sl].astype(F32), cos, sin, DIFF_PAIR) * q_scale).astype(qo_ref.dtype)
        ko_ref[:, sl] = _rotate(k_ref[:, sl].astype(F32), cos, sin, DIFF_PAIR).astype(ko_ref.dtype)


def rotary_qk(proj, gq, gk, cos, sin, batch, q_scale):
    t = proj.shape[0]
    s = t // batch
    tm = _tile(s, TILES["rotary"])
    ns = s // tm
    tab = pl.BlockSpec((tm, HEAD_DIM), lambda i: (i % ns, 0))
    out = pl.BlockSpec((tm, GROUP), lambda i: (i, 0))
    return pl.pallas_call(
        functools.partial(_rotary_kernel, q_scale=q_scale),
        grid=(t // tm,),
        in_specs=[pl.BlockSpec((tm, GROUP), lambda i: (i, gq)),
                  pl.BlockSpec((tm, GROUP), lambda i: (i, gk)), tab, tab],
        out_specs=[out, out],
        out_shape=[jax.ShapeDtypeStruct((t, GROUP), BF16)] * 2,
        compiler_params=_params("parallel"),
        name="rotary_qk",
    )(proj, proj, cos, sin)


def _online_softmax_step(s, v, m_sc, acc_sc):
    m_old = m_sc[...]
    m_new = jnp.maximum(m_old, jnp.max(s, axis=-1, keepdims=True))
    p = jnp.exp2(s - jnp.tile(m_new, (1, s.shape[1] // HEAD_DIM)))
    alpha = jnp.exp2(m_old - m_new)
    v_ones = jnp.concatenate([v, jnp.ones_like(v)], axis=1)
    acc_sc[...] = (jnp.tile(alpha, (1, 2)) * acc_sc[...]
                   + jnp.dot(p.astype(v.dtype), v_ones, preferred_element_type=F32))
    m_sc[...] = m_new


def _softmax_init(m_sc, acc_sc):
    m_sc[...] = jnp.full_like(m_sc, NEG_BIG)
    acc_sc[...] = jnp.zeros_like(acc_sc)


def _softmax_result(acc_sc):
    acc = acc_sc[...]
    return acc[:, :HEAD_DIM] / acc[:, HEAD_DIM:]


def _qk(q, k):
    return lax.dot_general(q, k, (((1,), (1,)), ((), ())), preferred_element_type=F32)


def _causal_sweep(qi, logits_fn, values_fn, tq, m_sc, acc_sc):
    def full(j):
        _online_softmax_step(logits_fn(j), values_fn(j), m_sc, acc_sc)

    def diagonal():
        s = logits_fn(qi)
        row = lax.broadcasted_iota(jnp.int32, s.shape, 0) & (tq - 1)
        col = lax.broadcasted_iota(jnp.int32, s.shape, 1)
        _online_softmax_step(jnp.where(col <= row, s, NEG_BIG), values_fn(qi), m_sc, acc_sc)

    def group(j, carry):
        for u in range(ATTN_GROUP):
            full(ATTN_GROUP * j + u)
        return carry

    _softmax_init(m_sc, acc_sc)
    lax.fori_loop(0, qi // ATTN_GROUP, group, 0)
    rest = qi % ATTN_GROUP
    for r in range(ATTN_GROUP):
        @pl.when(rest == r)
        def _(r=r):
            for u in range(r):
                full(qi - r + u)
            diagonal()


def _fox_kernel(q_ref, k_ref, v_ref, c_ref, o_ref, m_sc, acc_sc, *, tq, scale):
    qi = pl.program_id(2)
    q = (q_ref[...].astype(F32) * scale).astype(q_ref.dtype)
    c0 = c_ref[:, pl.ds(pl.multiple_of(qi * tq, tq), CHUNK)][:, 0:1]

    def logits(j):
        start = pl.multiple_of(j * tq, tq)
        return _qk(q, k_ref[pl.ds(start, tq), :]) + (c0 - c_ref[:, pl.ds(start, tq)]) * LOG2E

    def values(j):
        return v_ref[pl.ds(pl.multiple_of(j * tq, tq), tq), :]

    _causal_sweep(qi, logits, values, tq, m_sc, acc_sc)
    o_ref[...] = _softmax_result(acc_sc).astype(o_ref.dtype)


def fox_attention(proj, cum3, batch):
    t = proj.shape[0]
    s = t // batch
    tq = _tile(s, TILES["attn_q"])
    assert tq & (tq - 1) == 0, "the causal mask takes row mod tq with a bit mask"
    nq = s // tq
    return pl.pallas_call(
        functools.partial(_fox_kernel, tq=tq, scale=HEAD_DIM ** -0.5 * LOG2E),
        grid=(batch, HEADS, nq),
        in_specs=[pl.BlockSpec((tq, HEAD_DIM), lambda b, h, i: (b * nq + i, GA_FQ * HEADS + h)),
                  pl.BlockSpec((s, HEAD_DIM), lambda b, h, i: (b, GA_FK * HEADS + h)),
                  pl.BlockSpec((s, HEAD_DIM), lambda b, h, i: (b, GA_FV * HEADS + h)),
                  pl.BlockSpec((None, 1, s), lambda b, h, i: (h, 0, b))],
        out_specs=pl.BlockSpec((tq, HEAD_DIM), lambda b, h, i: (b * nq + i, h)),
        out_shape=jax.ShapeDtypeStruct((t, GROUP), BF16),
        scratch_shapes=[pltpu.VMEM((tq, HEAD_DIM), F32), pltpu.VMEM((tq, 2 * HEAD_DIM), F32)],
        compiler_params=_params("parallel", "parallel", "arbitrary"),
        name="fox_attention",
    )(proj, proj, proj, cum3)


def _diff_kernel(q_ref, k_ref, v_ref, lam_ref, g_ref, o_ref, m_sc, acc_sc, *, tq, lam_init):
    qi = pl.program_id(2)
    q = q_ref[...]
    lane = lax.broadcasted_iota(jnp.int32, q.shape, 1)
    first_map = lane < DIFF_DIM
    zero = jnp.zeros_like(q)
    q12 = jnp.concatenate([jnp.where(first_map, q, zero), jnp.where(first_map, zero, q)], axis=0)

    def logits(j):
        return _qk(q12, k_ref[pl.ds(pl.multiple_of(j * tq, tq), tq), :])

    def values(j):
        return v_ref[pl.ds(pl.multiple_of(j * tq, tq), tq), :]

    _causal_sweep(qi, logits, values, tq, m_sc, acc_sc)
    o = _softmax_result(acc_sc)
    lp = lam_ref[...]
    lam = (jnp.exp(jnp.sum(lp[0:1] * lp[1:2], axis=-1, keepdims=True))
           - jnp.exp(jnp.sum(lp[2:3] * lp[3:4], axis=-1, keepdims=True)) + lam_init)
    d = o[:tq] - lam * o[tq:]
    o_ref[...] = (_rms(d) * g_ref[...] * (1.0 - lam_init)).astype(o_ref.dtype)


def diff_attention(dq, dk, proj, lam_params, subln_g, batch, lam_init):
    t = proj.shape[0]
    s = t // batch
    tq = _tile(s, TILES["attn_q"])
    assert tq & (tq - 1) == 0, "the causal mask takes row mod tq with a bit mask"
    nq = s // tq
    return pl.pallas_call(
        functools.partial(_diff_kernel, tq=tq, lam_init=lam_init),
        grid=(batch, HEADS, nq),
        in_specs=[pl.BlockSpec((tq, HEAD_DIM), lambda b, h, i: (b * nq + i, h)),
                  pl.BlockSpec((s, HEAD_DIM), lambda b, h, i: (b, h)),
                  pl.BlockSpec((s, HEAD_DIM), lambda b, h, i: (b, GB_DV * HEADS + h)),
                  pl.BlockSpec((4, DIFF_DIM), lambda b, h, i: (0, 0)),
                  pl.BlockSpec((1, HEAD_DIM), lambda b, h, i: (0, 0))],
        out_specs=pl.BlockSpec((tq, HEAD_DIM), lambda b, h, i: (b * nq + i, h)),
        out_shape=jax.ShapeDtypeStruct((t, GROUP), BF16),
        scratch_shapes=[pltpu.VMEM((2 * tq, HEAD_DIM), F32), pltpu.VMEM((2 * tq, 2 * HEAD_DIM), F32)],
        compiler_params=_params("parallel", "parallel", "arbitrary"),
        name="diff_attention",
    )(dq, dk, proj, lam_params, subln_g.reshape(1, HEAD_DIM))


def _tn_dot(a, b):
    return lax.dot_general(a, b, (((0,), (0,)), ((), ())), preferred_element_type=F32)


def _retention_kernel(q_ref, k_ref, v_ref, g_ref, cos_ref, sin_ref, o_ref, state_sc, *, n_chunks):
    @pl.when(pl.program_id(2) == 0)
    def _():
        state_sc[...] = jnp.zeros_like(state_sc)

    rowi = lax.broadcasted_iota(jnp.int32, (CHUNK, CHUNK), 0)
    coli = lax.broadcasted_iota(jnp.int32, (CHUNK, CHUNK), 1)
    rel = (rowi - coli).astype(F32)
    pos = lax.broadcasted_iota(jnp.int32, (CHUNK, 1), 0).astype(F32)
    k_scale = HEAD_DIM ** -0.5

    decays = []
    for hh in range(HEADS_PER_STEP):
        h = (pl.program_id(1) * HEADS_PER_STEP + hh).astype(F32)
        log_gamma = jnp.log(1.0 - jnp.exp2(jnp.full((1, 1), -5.0, F32) - h))
        decays.append((jnp.where(rel >= 0, jnp.exp(jnp.maximum(rel, 0.0) * log_gamma), 0.0),
                       jnp.exp((pos + 1.0) * log_gamma),
                       jnp.exp((CHUNK - 1.0 - pos) * log_gamma),
                       jnp.exp(CHUNK * log_gamma)))

    states = [state_sc[hh] for hh in range(HEADS_PER_STEP)]
    for c in range(n_chunks):
        rows = slice(c * CHUNK, (c + 1) * CHUNK)
        cos, sin = cos_ref[rows, :], sin_ref[rows, :]
        for hh in range(HEADS_PER_STEP):
            intra, q_decay, k_decay, chunk_decay = decays[hh]
            cols = slice(hh * HEAD_DIM, (hh + 1) * HEAD_DIM)
            q = _rotate(q_ref[rows, cols].astype(F32), cos, sin, RET_PAIR).astype(BF16)
            kf = _rotate(k_ref[rows, cols].astype(F32), cos, sin, RET_PAIR) * k_scale
            v = v_ref[rows, cols]
            scores = _qk(q, kf.astype(BF16)) * intra
            o = (jnp.dot(scores.astype(BF16), v, preferred_element_type=F32)
                 + jnp.dot(q, states[hh].astype(BF16), preferred_element_type=F32) * q_decay)
            states[hh] = chunk_decay * states[hh] + _tn_dot((kf * k_decay).astype(BF16), v)
            g = g_ref[rows, cols].astype(F32)
            o_ref[rows, cols] = (g * jax.nn.sigmoid(g) * _rms(o)).astype(o_ref.dtype)
    for hh in range(HEADS_PER_STEP):
        state_sc[hh] = states[hh]


def _head_cols(tb, nb, group):
    width = HEADS_PER_STEP * HEAD_DIM
    return pl.BlockSpec((tb, width), lambda b, hp, i: (b * nb + i, group * (GROUP // width) + hp))


def retention(proj, cos, sin, batch):
    t = proj.shape[0]
    s = t // batch
    tb = _tile(s, TILES["recurrent"])
    nb = s // tb
    tab = pl.BlockSpec((tb, HEAD_DIM), lambda b, hp, i: (i, 0))
    return pl.pallas_call(
        functools.partial(_retention_kernel, n_chunks=tb // CHUNK),
        grid=(batch, HEADS // HEADS_PER_STEP, nb),
        in_specs=[_head_cols(tb, nb, GB_RQ), _head_cols(tb, nb, GB_RK), _head_cols(tb, nb, GB_RV),
                  _head_cols(tb, nb, GB_RG), tab, tab],
        out_specs=_head_cols(tb, nb, 0),
        out_shape=jax.ShapeDtypeStruct((t, GROUP), BF16),
        scratch_shapes=[pltpu.VMEM((HEADS_PER_STEP, HEAD_DIM, HEAD_DIM), F32)],
        compiler_params=_params("parallel", "parallel", "arbitrary"),
        name="retention",
    )(proj, proj, proj, proj, cos, sin)


def _causal_conv_silu(x, prev, w):
    tb = x.shape[0]
    ext = jnp.concatenate([prev, x], axis=0)
    acc = None
    for i in range(CONV_WIDTH):
        off = 8 - (CONV_WIDTH - 1) + i
        term = ext[off:off + tb, :] * w[i:i + 1, :]
        acc = term if acc is None else acc + term
    return acc * jax.nn.sigmoid(acc)


def _mlstm_kernel(q_ref, k_ref, v_ref, og_ref, wq_ref, wk_ref, li_ref, bc_ref, gcol_ref, ng_ref,
                  o_ref, c_sc, n_sc, m_sc, pq_sc, pk_sc, *, n_chunks):
    @pl.when(pl.program_id(2) == 0)
    def _():
        c_sc[...] = jnp.zeros_like(c_sc)
        n_sc[...] = jnp.zeros_like(n_sc)
        m_sc[...] = jnp.zeros_like(m_sc)
        pq_sc[...] = jnp.zeros_like(pq_sc)
        pk_sc[...] = jnp.zeros_like(pk_sc)

    tb = q_ref.shape[0]
    xq = q_ref[...].astype(F32)
    xk = k_ref[...].astype(F32)
    qf_all = _causal_conv_silu(xq, pq_sc[...], wq_ref[...])
    kf_all = _causal_conv_silu(xk, pk_sc[...], wk_ref[...]) * (HEAD_DIM ** -0.5)
    pq_sc[...] = xq[tb - 8:, :]
    pk_sc[...] = xk[tb - 8:, :]

    lane = lax.broadcasted_iota(jnp.int32, (CHUNK, GATE_ROWS), 1)
    rowi = lax.broadcasted_iota(jnp.int32, (CHUNK, CHUNK), 0)
    coli = lax.broadcasted_iota(jnp.int32, (CHUNK, CHUNK), 1)
    causal = coli <= rowi

    carries = [(c_sc[hh], n_sc[hh], m_sc[hh]) for hh in range(HEADS_PER_STEP)]
    for c in range(n_chunks):
        rows = slice(c * CHUNK, (c + 1) * CHUNK)
        gc = gcol_ref[rows, :]
        for hh in range(HEADS_PER_STEP):
            h = pl.program_id(1) * HEADS_PER_STEP + hh
            c_mat, n_vec, m = carries[hh]
            cols = slice(hh * HEAD_DIM, (hh + 1) * HEAD_DIM)
            qf, kf = qf_all[rows, cols], kf_all[rows, cols]
            q, k, v = qf.astype(BF16), kf.astype(BF16), v_ref[rows, cols]
            li_row, bc_row = li_ref[hh, :, rows], bc_ref[hh, :, rows]
            li_col = jnp.sum(jnp.where(lane == HEADS + h, gc, 0.0), axis=1, keepdims=True)
            bc_col = jnp.sum(jnp.where(lane == 2 * HEADS + h, gc, 0.0), axis=1, keepdims=True)
            b_last = bc_row[:, CHUNK - 1:CHUNK]

            dlog = jnp.where(causal, bc_col - bc_row + li_row, -jnp.inf)
            inter = bc_col + m
            m_t = jnp.maximum(inter, jnp.max(dlog, axis=-1, keepdims=True))
            w_intra = jnp.exp(dlog - m_t)
            w_inter = jnp.exp(inter - m_t)
            s_qk = _qk(q, k) * w_intra
            num = (jnp.dot(s_qk.astype(BF16), v, preferred_element_type=F32)
                   + w_inter * jnp.dot(q, c_mat.astype(BF16), preferred_element_type=F32))
            den = (jnp.sum(s_qk, axis=-1, keepdims=True)
                   + w_inter * jnp.sum(qf * n_vec, axis=-1, keepdims=True))
            h_t = num / jnp.maximum(jnp.abs(den), jnp.exp(-m_t))

            end_log = b_last - bc_col + li_col
            m_new = jnp.maximum(b_last + m, jnp.max(end_log, axis=0, keepdims=True))
            w_end = jnp.exp(end_log - m_new)
            carry_decay = jnp.exp(b_last + m - m_new)
            kw = kf * w_end
            carries[hh] = (carry_decay * c_mat + _tn_dot(kw.astype(BF16), v),
                           carry_decay * n_vec + jnp.sum(kw, axis=0, keepdims=True),
                           m_new)

            og = og_ref[rows, cols].astype(F32)
            o_ref[rows, cols] = (jax.nn.sigmoid(og) * (_rms(h_t) * ng_ref[:, cols])).astype(o_ref.dtype)
    for hh in range(HEADS_PER_STEP):
        c_sc[hh], n_sc[hh], m_sc[hh] = carries[hh]


def mlstm(proj, conv_w, gates3, gates_col, norm_g, batch):
    t = proj.shape[0]
    s = t // batch
    tb = _tile(s, TILES["recurrent"])
    nb = s // tb
    hps = HEADS_PER_STEP
    width = hps * HEAD_DIM
    n_hp = HEADS // hps
    return pl.pallas_call(
        functools.partial(_mlstm_kernel, n_chunks=tb // CHUNK),
        grid=(batch, n_hp, nb),
        in_specs=[_head_cols(tb, nb, GB_MQ), _head_cols(tb, nb, GB_MK), _head_cols(tb, nb, GB_MV),
                  _head_cols(tb, nb, GB_MO),
                  pl.BlockSpec((CONV_WIDTH, width), lambda b, hp, i: (0, hp)),
                  pl.BlockSpec((CONV_WIDTH, width), lambda b, hp, i: (0, n_hp + hp)),
                  pl.BlockSpec((hps, 1, tb), lambda b, hp, i: (n_hp + hp, 0, b * nb + i)),
                  pl.BlockSpec((hps, 1, tb), lambda b, hp, i: (2 * n_hp + hp, 0, b * nb + i)),
                  pl.BlockSpec((tb, GATE_ROWS), lambda b, hp, i: (b * nb + i, 0)),
                  pl.BlockSpec((1, width), lambda b, hp, i: (0, hp))],
        out_specs=_head_cols(tb, nb, 0),
        out_shape=jax.ShapeDtypeStruct((t, GROUP), BF16),
        scratch_shapes=[pltpu.VMEM((hps, HEAD_DIM, HEAD_DIM), F32), pltpu.VMEM((hps, 1, HEAD_DIM), F32),
                        pltpu.VMEM((hps, 1, 1), F32), pltpu.VMEM((8, width), F32),
                        pltpu.VMEM((8, width), F32)],
        compiler_params=_params("parallel", "parallel", "arbitrary"),
        name="mlstm",
    )(proj, proj, proj, proj, conv_w, conv_w, gates3, gates3, gates_col, norm_g.reshape(1, GROUP))


W_IN_A = N_MIXERS - 1
W_IN_B = 11
W_IN_B_START = W_IN_A * GROUP + HEADS
W_IN_GATES = W_IN_B_START + W_IN_B * GROUP


def _prep_w_in(w_in_l):
    wide_a = w_in_l[:, :W_IN_A * GROUP].astype(BF16)
    wide_b = w_in_l[:, W_IN_B_START:W_IN_GATES].astype(BF16)
    d = w_in_l.shape[0]
    gate_t = jnp.concatenate([w_in_l[:, W_IN_A * GROUP:W_IN_B_START], w_in_l[:, W_IN_GATES:],
                              jnp.zeros((d, GATE_ROWS - 3 * HEADS), w_in_l.dtype)], axis=1).T.astype(BF16)
    return wide_a, wide_b, gate_t


def _rotary_tables(seq):
    pos = jnp.arange(seq, dtype=F32)[:, None]
    inv_d = 1.0 / (ROPE_THETA ** (jnp.arange(0, DIFF_DIM, 2, dtype=F32) / DIFF_DIM))
    ang_d = pos * inv_d[None, :]
    cos_d = jnp.tile(jnp.cos(ang_d), (1, 4))
    sin_d = jnp.tile(jnp.concatenate([-jnp.sin(ang_d), jnp.sin(ang_d)], axis=1), (1, 2))
    inv_r = 1.0 / (RET_THETA ** jnp.linspace(0.0, 1.0, HEAD_DIM // 2, dtype=F32))
    ang_r = pos * inv_r[None, :]
    cos_r = jnp.repeat(jnp.cos(ang_r), 2, axis=1)
    sin_r = jnp.stack([-jnp.sin(ang_r), jnp.sin(ang_r)], axis=-1).reshape(seq, HEAD_DIM)
    return cos_d, sin_d, cos_r, sin_r


def kernel(x, norm_gains, ffn_a_gate, ffn_a_up, ffn_a_down, w_in, w_out, fox_fgate_b,
           diff_lq1, diff_lk1, diff_lq2, diff_lk2, diff_subln_g, mlstm_conv_w,
           mlstm_igate_b, mlstm_fgate_b, mlstm_norm_g, ffn_b_gate, ffn_b_up, ffn_b_down):
    batch, seq, d_model = x.shape
    depth = norm_gains.shape[0]
    t = batch * seq
    assert d_model == N_MIXERS * GROUP and seq % CHUNK == 0
    cos_d, sin_d, cos_r, sin_r = _rotary_tables(seq)
    w_a_down, w_b_down = ffn_a_down.astype(BF16), ffn_b_down.astype(BF16)
    w_out4 = w_out.astype(BF16).reshape(depth, N_MIXERS, GROUP, d_model)

    xr = x.reshape(t, d_model)
    xn = rms_norm_rows(xr, norm_gains[0, 0])
    for l in range(depth):
        lam_init = 0.8 - 0.6 * math.exp(-0.3 * l)
        g = norm_gains[l]
        last = l == depth - 1

        hmid = ffn_up(xn, ffn_a_gate, ffn_a_up, l)
        y = matmul(hmid, w_a_down, "ffn_down", "ffn_down", layer=l)
        xr, xn = resid_norm(xr, y, g[1], g[2], 0.5)

        w_wide_a, w_wide_b, w_gate_t = _prep_w_in(w_in[l])
        proj_a = matmul(xn, w_wide_a, "in_proj", "in_proj_fox")
        proj_b = matmul(xn, w_wide_b, "in_proj", "in_proj")
        gate_bias = jnp.concatenate([fox_fgate_b[l], mlstm_igate_b[l], mlstm_fgate_b[l],
                                     jnp.zeros((GATE_ROWS - 3 * HEADS,), F32)]).reshape(GATE_ROWS, 1)
        gts = gates(xn, w_gate_t, gate_bias, batch)
        gates3 = gts.reshape(GATE_ROWS, 1, t)
        gates_col = gts.T

        fox_out = fox_attention(proj_a, gates3, batch)
        dq, dk = rotary_qk(proj_b, GB_DQ, GB_DK, cos_d, sin_d, batch, DIFF_DIM ** -0.5 * LOG2E)
        lam_params = jnp.stack([diff_lq1[l], diff_lk1[l], diff_lq2[l], diff_lk2[l]]).astype(F32)
        diff_out = diff_attention(dq, dk, proj_b, lam_params, diff_subln_g[l], batch, lam_init)
        ret_out = retention(proj_b, cos_r, sin_r, batch)
        mlstm_out = mlstm(proj_b, mlstm_conv_w[l], gates3, gates_col, mlstm_norm_g[l], batch)

        mix = out_proj([fox_out, diff_out, ret_out, mlstm_out], w_out4, l)
        xr, xn = resid_norm(xr, mix, g[3], g[4], 1.0)

        hmid = ffn_up(xn, ffn_b_gate, ffn_b_up, l)
        y = matmul(hmid, w_b_down, "ffn_down", "ffn_down", layer=l)
        xr, xn = resid_norm(xr, y, g[5], None if last else norm_gains[l + 1, 0], 0.5)
    return xr.reshape(batch, seq, d_model)
```
